```python
import math
import jax, jax.numpy as jnp
from jax import lax
import numpy as np

D_MODEL = 1024
BATCH = 2
SEQ = 16384
DEPTH = 1

N_MEM = 256
MIX_WIDTH = D_MODEL
DA_WIDTH = MIX_WIDTH // 2
CV_WIDTH = MIX_WIDTH - DA_WIDTH
DA_HEADS = 4
DA_VDIM = DA_WIDTH // DA_HEADS
DA_QKDIM = DA_VDIM // 2
ROT_DIM = DA_QKDIM // 4
ROPE_THETA = 500000.0
Q_BLOCK = 128
CV_KERNEL = 31
X_HEADS = 4
X_HEAD_DIM = D_MODEL // X_HEADS
D_FF = 2816
FFN_KERNEL = 3
EPS = 1e-6

Q_COLS = DA_HEADS * 2 * DA_QKDIM
K_COLS = DA_HEADS * 2 * DA_QKDIM
V_COLS = DA_HEADS * DA_VDIM
CV_COLS = 2 * CV_WIDTH
IN_COLS = Q_COLS + K_COLS + V_COLS + CV_COLS

kernel_name = "hybrid_diffattn_conformer_convffn_block"


def rmsnorm(x, g):
    xf = x.astype(jnp.float32)
    y = xf * lax.rsqrt(jnp.mean(xf * xf, axis=-1, keepdims=True) + EPS)
    return (y * g.astype(jnp.float32)).astype(x.dtype)


def layernorm(x, g, b):
    xf = x.astype(jnp.float32)
    mu = jnp.mean(xf, axis=-1, keepdims=True)
    var = jnp.mean(jnp.square(xf - mu), axis=-1, keepdims=True)
    y = (xf - mu) * lax.rsqrt(var + EPS)
    return (y * g.astype(jnp.float32) + b.astype(jnp.float32)).astype(x.dtype)


def causal_dwconv(x, w):
    k, c = w.shape
    return lax.conv_general_dilated(
        x, w[:, None, :].astype(x.dtype), window_strides=(1,),
        padding=[(k - 1, 0)], dimension_numbers=("NWC", "WIO", "NWC"),
        feature_group_count=c)


def partial_rope(t, positions):
    half = ROT_DIM // 2
    inv_freq = ROPE_THETA ** (-jnp.arange(0, ROT_DIM, 2, dtype=jnp.float32) / ROT_DIM)
    ang = positions.astype(jnp.float32)[..., None] * inv_freq
    cos = jnp.cos(ang)[:, :, None, None, :]
    sin = jnp.sin(ang)[:, :, None, None, :]
    tf = t.astype(jnp.float32)
    t1, t2, rest = tf[..., :half], tf[..., half:ROT_DIM], tf[..., ROT_DIM:]
    out = jnp.concatenate([t1 * cos - t2 * sin, t2 * cos + t1 * sin, rest], axis=-1)
    return out.astype(t.dtype)


def diff_attention(q, k, v, lam):
    b, h, _, s, d = q.shape
    dv = v.shape[-1]
    nblk = s // Q_BLOCK
    scale = 1.0 / math.sqrt(d)
    qb = q.reshape(b, h, 2, nblk, Q_BLOCK, d).transpose(3, 0, 1, 2, 4, 5)
    kpos = jnp.arange(s)

    def one_block(args):
        q_blk, i = args
        sc = jnp.einsum("bhcqd,bhckd->bhcqk", q_blk, k).astype(jnp.float32) * scale
        qpos = i * Q_BLOCK + jnp.arange(Q_BLOCK)
        mask = kpos[None, :] <= qpos[:, None]
        sc = jnp.where(mask, sc, -jnp.inf)
        p = jax.nn.softmax(sc, axis=-1)
        a = p[:, :, 0] - lam * p[:, :, 1]
        return jnp.einsum("bhqk,bhkv->bhqv", a.astype(v.dtype), v)

    out = lax.map(one_block, (qb, jnp.arange(nblk)))
    return out.transpose(1, 0, 3, 2, 4).reshape(b, s, h, dv)


def hybrid_mixer(h, positions, layer_idx, w_in, lam_q1, lam_k1, lam_q2, lam_k2,
                 subln_g, cv_dw_w, cv_dw_b, cv_ln_g, cv_ln_b, w_out):
    b, s, _ = h.shape
    proj = jnp.einsum("bsd,dc->bsc", h, w_in)
    o1 = Q_COLS
    o2 = o1 + K_COLS
    o3 = o2 + V_COLS
    q = proj[..., :o1].reshape(b, s, DA_HEADS, 2, DA_QKDIM)
    k = proj[..., o1:o2].reshape(b, s, DA_HEADS, 2, DA_QKDIM)
    v = proj[..., o2:o3].reshape(b, s, DA_HEADS, DA_VDIM)
    u = proj[..., o3:]

    q = partial_rope(q, positions).transpose(0, 2, 3, 1, 4)
    k = partial_rope(k, positions).transpose(0, 2, 3, 1, 4)
    v = v.transpose(0, 2, 1, 3)
    lam_init = 0.8 - 0.6 * math.exp(-0.3 * layer_idx)
    lam = (jnp.exp(jnp.sum(lam_q1.astype(jnp.float32) * lam_k1.astype(jnp.float32)))
           - jnp.exp(jnp.sum(lam_q2.astype(jnp.float32) * lam_k2.astype(jnp.float32)))
           + lam_init)
    a = diff_attention(q, k, v, lam)
    a = rmsnorm(a, subln_g) * (1.0 - lam_init)
    a = a.reshape(b, s, DA_WIDTH)

    c = u[..., :CV_WIDTH] * jax.nn.sigmoid(u[..., CV_WIDTH:])
    c = causal_dwconv(c, cv_dw_w) + cv_dw_b.astype(c.dtype)
    c = jax.nn.silu(layernorm(c, cv_ln_g, cv_ln_b))

    y = jnp.concatenate([a, c], axis=-1)
    return jnp.einsum("bsc,cd->bsd", y, w_out)


def cross_attention(h, m, w_cq, w_ckv, w_co):
    b, s, _ = h.shape
    q = jnp.einsum("bsd,de->bse", h, w_cq).reshape(b, s, X_HEADS, X_HEAD_DIM)
    kv = jnp.einsum("bmd,de->bme", m, w_ckv)
    k = kv[..., :D_MODEL].reshape(b, m.shape[1], X_HEADS, X_HEAD_DIM)
    v = kv[..., D_MODEL:].reshape(b, m.shape[1], X_HEADS, X_HEAD_DIM)
    sc = jnp.einsum("bshd,bmhd->bhsm", q, k).astype(jnp.float32) / math.sqrt(X_HEAD_DIM)
    p = jax.nn.softmax(sc, axis=-1).astype(v.dtype)
    o = jnp.einsum("bhsm,bmhd->bshd", p, v).reshape(b, s, D_MODEL)
    return jnp.einsum("bse,ed->bsd", o, w_co)


def conv_ffn(h, w_up, ffn_dw_w, w_down):
    up = jnp.einsum("bsd,df->bsf", h, w_up)
    up = causal_dwconv(up, ffn_dw_w)
    z = jax.nn.silu(up[..., :D_FF]) * up[..., D_FF:]
    return jnp.einsum("bsf,fd->bsd", z, w_down)


def setup_inputs(seed: int = 0) -> dict:
    key = jax.random.key(seed)
    ks = jax.random.split(key, 24)
    f32 = jnp.float32

    def nrm(k, shape, scale):
        return jax.random.normal(k, shape, f32) * scale

    def gain(k, shape):
        return 1.0 + 0.02 * jax.random.normal(k, shape, f32)

    L = DEPTH
    return {
        "x": jax.random.normal(ks[0], (BATCH, SEQ, D_MODEL), f32),
        "mem": jax.random.normal(ks[1], (BATCH, N_MEM, D_MODEL), f32),
        "positions": jnp.broadcast_to(jnp.arange(SEQ, dtype=jnp.int32), (BATCH, SEQ)),
        "norm_mix_g": gain(ks[2], (L, D_MODEL)),
        "w_in": nrm(ks[3], (L, D_MODEL, IN_COLS), D_MODEL ** -0.5),
        "lam_q1": nrm(ks[4], (L, DA_QKDIM), 0.1),
        "lam_k1": nrm(ks[5], (L, DA_QKDIM), 0.1),
        "lam_q2": nrm(ks[6], (L, DA_QKDIM), 0.1),
        "lam_k2": nrm(ks[7], (L, DA_QKDIM), 0.1),
        "subln_g": gain(ks[8], (L, DA_VDIM)),
        "cv_dw_w": nrm(ks[9], (L, CV_KERNEL, CV_WIDTH), CV_KERNEL ** -0.5),
        "cv_dw_b": nrm(ks[10], (L, CV_WIDTH), 0.02),
        "cv_ln_g": gain(ks[11], (L, CV_WIDTH)),
        "cv_ln_b": nrm(ks[12], (L, CV_WIDTH), 0.02),
        "w_out": nrm(ks[13], (L, MIX_WIDTH, D_MODEL), MIX_WIDTH ** -0.5),
        "norm_cross_g": gain(ks[14], (L, D_MODEL)),
        "norm_mem_g": gain(ks[15], (L, D_MODEL)),
        "w_cq": nrm(ks[16], (L, D_MODEL, D_MODEL), D_MODEL ** -0.5),
        "w_ckv": nrm(ks[17], (L, D_MODEL, 2 * D_MODEL), D_MODEL ** -0.5),
        "w_co": nrm(ks[18], (L, D_MODEL, D_MODEL), D_MODEL ** -0.5),
        "norm_ffn_g": gain(ks[19], (L, D_MODEL)),
        "w_up": nrm(ks[20], (L, D_MODEL, 2 * D_FF), D_MODEL ** -0.5),
        "ffn_dw_w": nrm(ks[21], (L, FFN_KERNEL, 2 * D_FF), FFN_KERNEL ** -0.5),
        "w_down": nrm(ks[22], (L, D_FF, D_MODEL), D_FF ** -0.5),
        "norm_final_g": gain(ks[23], (D_MODEL,)),
    }


def reference(x, mem, positions, norm_mix_g, w_in, lam_q1, lam_k1, lam_q2, lam_k2,
              subln_g, cv_dw_w, cv_dw_b, cv_ln_g, cv_ln_b, w_out,
              norm_cross_g, norm_mem_g, w_cq, w_ckv, w_co,
              norm_ffn_g, w_up, ffn_dw_w, w_down, norm_final_g):
    h = x
    for l in range(DEPTH):
        h = h + hybrid_mixer(rmsnorm(h, norm_mix_g[l]), positions, l, w_in[l],
                             lam_q1[l], lam_k1[l], lam_q2[l], lam_k2[l], subln_g[l],
                             cv_dw_w[l], cv_dw_b[l], cv_ln_g[l], cv_ln_b[l], w_out[l])
        h = h + cross_attention(rmsnorm(h, norm_cross_g[l]), rmsnorm(mem, norm_mem_g[l]),
                                w_cq[l], w_ckv[l], w_co[l])
        h = h + conv_ffn(rmsnorm(h, norm_ffn_g[l]), w_up[l], ffn_dw_w[l], w_down[l])
    return rmsnorm(h, norm_final_g)
```

```python
import functools
import math

import jax
import jax.numpy as jnp
from jax import lax
from jax.experimental import pallas as pl
from jax.experimental.pallas import tpu as pltpu

F32 = jnp.float32
BF16 = jnp.bfloat16

N_DA_HEADS = 4
DA_VDIM = 128
DA_QKDIM = 64
ROT_DIM = 16
ROPE_THETA = 500000.0
CV_KERNEL = 31
N_X_HEADS = 4
FFN_KERNEL = 3
EPS = 1e-6
LAM_INIT = 0.8 - 0.6 * math.exp(-0.3 * 0)

LANES = 128
SUBLANES = 8
VMEM_LIMIT_BYTES = 56 * 1024 * 1024

NEG_BIG = -1e30

TOK_BLOCK = 512
ATT_Q_BLOCK = 256
ATT_K_BLOCK = 256
CV_HALO = 32
CV_ROW_TILE = 32
FFN_CHUNK = 256


def _rms(x, g):
    ms = jnp.mean(x * x, axis=-1, keepdims=True)
    return x * lax.rsqrt(ms + EPS) * g


def _inproj_kernel(x_ref, pos_ref, g_ref, w_ref, freq_ref, ma_ref, mb_ref,
                   q_ref, k_ref, v_ref, c_ref):
    da = N_DA_HEADS * DA_VDIM
    xn = _rms(x_ref[0], g_ref[...]).astype(BF16)
    ang = pos_ref[0].astype(F32) * freq_ref[...]
    cos = jnp.cos(ang)
    sin = jnp.sin(ang)
    sa = sin * ma_ref[...]
    sb = sin * mb_ref[...]

    def rope(t):
        return (t * cos + pltpu.roll(t, ROT_DIM // 2, 1) * sa
                + pltpu.roll(t, LANES - ROT_DIM // 2, 1) * sb)

    qp = jnp.dot(xn, w_ref[:, 0:da], preferred_element_type=F32)
    kp = jnp.dot(xn, w_ref[:, da:2 * da], preferred_element_type=F32)
    scale = 1.0 / math.sqrt(DA_QKDIM)
    for h in range(N_DA_HEADS):
        sl = slice(h * LANES, (h + 1) * LANES)
        q_ref[0, :, sl] = (rope(qp[:, sl]) * scale).astype(BF16)
        k_ref[0, :, sl] = rope(kp[:, sl]).astype(BF16)
    v_ref[0] = jnp.dot(xn, w_ref[:, 2 * da:3 * da], preferred_element_type=F32).astype(BF16)
    cw = (w_ref.shape[1] - 3 * da) // 2
    uv = jnp.dot(xn, w_ref[:, 3 * da:3 * da + cw], preferred_element_type=F32)
    ug = jnp.dot(xn, w_ref[:, 3 * da + cw:], preferred_element_type=F32)
    c_ref[0] = uv * jax.nn.sigmoid(ug)


def _inproj(x, positions, g, w_in):
    b, s, d = x.shape
    t = min(TOK_BLOCK, s)
    da = N_DA_HEADS * DA_VDIM
    cw = (w_in.shape[1] - 3 * da) // 2
    inv_freq = ROPE_THETA ** (-jnp.arange(0, ROT_DIM, 2, dtype=F32) / ROT_DIM)
    j = jnp.arange(LANES) % DA_QKDIM
    half = ROT_DIM // 2
    freq = jnp.where(j < ROT_DIM, inv_freq[j % half], 0.0).astype(F32)[None, :]
    ma = jnp.where((j >= half) & (j < ROT_DIM), 1.0, 0.0).astype(F32)[None, :]
    mb = jnp.where(j < half, -1.0, 0.0).astype(F32)[None, :]
    const = lambda shape: pl.BlockSpec(shape, lambda bi, i: (0,) * len(shape))
    tok = lambda w: pl.BlockSpec((1, t, w), lambda bi, i: (bi, i, 0))
    return pl.pallas_call(
        _inproj_kernel,
        grid=(b, s // t),
        in_specs=[tok(d), tok(1), const((1, d)), const(w_in.shape),
                  const((1, LANES)), const((1, LANES)), const((1, LANES))],
        out_specs=[tok(da), tok(da), tok(da), tok(cw)],
        out_shape=[jax.ShapeDtypeStruct((b, s, da), BF16)] * 3
                  + [jax.ShapeDtypeStruct((b, s, cw), F32)],
        compiler_params=pltpu.CompilerParams(
            dimension_semantics=("parallel", "parallel"), vmem_limit_bytes=VMEM_LIMIT_BYTES),
        name="inproj",
    )(x, positions.reshape(b, s, 1), g.reshape(1, d), w_in.astype(BF16), freq, ma, mb)


def _memkv_kernel(m_ref, g_ref, w_ref, k_ref, v_ref):
    d = m_ref.shape[2]
    mn = _rms(m_ref[0], g_ref[...]).astype(BF16)
    kv = jnp.dot(mn, w_ref[...], preferred_element_type=F32)
    scale = 1.0 / math.sqrt(d // N_X_HEADS)
    k_ref[0] = (kv[:, :d] * scale).astype(BF16)
    v_ref[0] = kv[:, d:].astype(BF16)


def _memkv(mem, g, w_ckv):
    b, m, d = mem.shape
    return pl.pallas_call(
        _memkv_kernel,
        grid=(b,),
        in_specs=[pl.BlockSpec((1, m, d), lambda bi: (bi, 0, 0)),
                  pl.BlockSpec((1, d), lambda bi: (0, 0)),
                  pl.BlockSpec((d, 2 * d), lambda bi: (0, 0))],
        out_specs=[pl.BlockSpec((1, m, d), lambda bi: (bi, 0, 0))] * 2,
        out_shape=[jax.ShapeDtypeStruct((b, m, d), BF16)] * 2,
        compiler_params=pltpu.CompilerParams(
            dimension_semantics=("parallel",), vmem_limit_bytes=VMEM_LIMIT_BYTES),
        name="memkv",
    )(mem, g.reshape(1, d), w_ckv.astype(BF16))


def _attn_kernel(q_ref, k_ref, v_ref, lq1_ref, lk1_ref, lq2_ref, lk2_ref, g_ref, o_ref,
                 m_scr, l_scr, acc_scr):
    tq, tk = ATT_Q_BLOCK, ATT_K_BLOCK
    qi = pl.program_id(2)
    q = q_ref[0]
    lane = lax.broadcasted_iota(jnp.int32, (tq, LANES), 1)
    zero = jnp.zeros_like(q)
    qs = jnp.concatenate([jnp.where(lane < DA_QKDIM, q, zero),
                          jnp.where(lane >= DA_QKDIM, q, zero)], axis=0)

    m_scr[...] = jnp.full(m_scr.shape, NEG_BIG, F32)
    l_scr[...] = jnp.zeros(l_scr.shape, F32)
    acc_scr[...] = jnp.zeros(acc_scr.shape, F32)

    def step(j, masked):
        k0 = pl.multiple_of(j * tk, tk)
        kc = k_ref[0, pl.ds(k0, tk), :]
        vc = v_ref[0, pl.ds(k0, tk), :]
        s = lax.dot_general(qs, kc, (((1,), (1,)), ((), ())),
                            preferred_element_type=F32)
        if masked:
            row = lax.broadcasted_iota(jnp.int32, (2 * tq, tk), 0) % tq
            col = lax.broadcasted_iota(jnp.int32, (2 * tq, tk), 1)
            s = jnp.where(col <= row, s, NEG_BIG)
        m_old = m_scr[...]
        m_new = jnp.maximum(m_old, jnp.max(s, axis=1, keepdims=True))
        alpha = jnp.exp(m_old - m_new)
        p = jnp.exp(s - m_new)
        l_scr[...] = alpha * l_scr[...] + jnp.sum(p, axis=1, keepdims=True)
        acc_scr[...] = alpha * acc_scr[...] + jnp.dot(p.astype(BF16), vc,
                                                      preferred_element_type=F32)
        m_scr[...] = m_new

    def body(j, carry):
        step(j, masked=False)
        return carry

    lax.fori_loop(0, qi * (tq // tk), body, 0)
    step(qi, masked=True)

    lam = (jnp.exp(jnp.sum(lq1_ref[...] * lk1_ref[...], keepdims=True))
           - jnp.exp(jnp.sum(lq2_ref[...] * lk2_ref[...], keepdims=True)) + LAM_INIT)
    o = acc_scr[...] / l_scr[...]
    a = o[:tq] - lam * o[tq:]
    o_ref[0] = (_rms(a, g_ref[...]) * (1.0 - LAM_INIT)).astype(o_ref.dtype)


def _diff_attention(q, k, v, lq1, lk1, lq2, lk2, subln_g):
    b, s, _ = q.shape
    tq = ATT_Q_BLOCK
    assert ATT_Q_BLOCK == ATT_K_BLOCK and s % tq == 0
    vec = lambda n: pl.BlockSpec((1, n), lambda bi, h, i: (0, 0))
    return pl.pallas_call(
        _attn_kernel,
        grid=(b, N_DA_HEADS, s // tq),
        in_specs=[pl.BlockSpec((1, tq, LANES), lambda bi, h, i: (bi, i, h)),
                  pl.BlockSpec((1, s, LANES), lambda bi, h, i: (bi, 0, h)),
                  pl.BlockSpec((1, s, DA_VDIM), lambda bi, h, i: (bi, 0, h)),
                  vec(DA_QKDIM), vec(DA_QKDIM), vec(DA_QKDIM), vec(DA_QKDIM), vec(DA_VDIM)],
        out_specs=pl.BlockSpec((1, tq, DA_VDIM), lambda bi, h, i: (bi, i, h)),
        out_shape=jax.ShapeDtypeStruct((b, s, N_DA_HEADS * DA_VDIM), BF16),
        scratch_shapes=[pltpu.VMEM((2 * tq, 1), F32), pltpu.VMEM((2 * tq, 1), F32),
                        pltpu.VMEM((2 * tq, DA_VDIM), F32)],
        compiler_params=pltpu.CompilerParams(
            dimension_semantics=("parallel", "parallel", "arbitrary"),
            vmem_limit_bytes=VMEM_LIMIT_BYTES),
        name="diffattn",
    )(q, k, v, lq1.reshape(1, -1), lk1.reshape(1, -1), lq2.reshape(1, -1), lk2.reshape(1, -1),
      subln_g.reshape(1, -1))


def _mixcross_kernel(x_ref, a_ref, c_ref, halo_ref, cw_ref, cb_ref, lg_ref, lb_ref, wo_ref,
                     ng_ref, wq_ref, mk_ref, mv_ref, wco_ref, o_ref, buf, cact):
    t = x_ref.shape[1]
    d = x_ref.shape[2]
    i = pl.program_id(1)
    halo = halo_ref[0]
    buf[0:CV_HALO, :] = jnp.where(i > 0, halo, jnp.zeros_like(halo))
    buf[CV_HALO:, :] = c_ref[0]

    base = CV_HALO - (CV_KERNEL - 1)
    for r0 in range(0, t, CV_ROW_TILE):
        acc = jnp.zeros((CV_ROW_TILE, buf.shape[1]), F32) + cb_ref[...]
        for j in range(CV_KERNEL):
            acc = acc + cw_ref[j:j + 1, :] * buf[r0 + base + j:r0 + base + j + CV_ROW_TILE, :]
        mu = jnp.mean(acc, axis=-1, keepdims=True)
        xc = acc - mu
        var = jnp.mean(xc * xc, axis=-1, keepdims=True)
        y = xc * lax.rsqrt(var + EPS) * lg_ref[...] + lb_ref[...]
        cact[r0:r0 + CV_ROW_TILE, :] = (y * jax.nn.sigmoid(y)).astype(BF16)

    da = a_ref.shape[2]
    h1 = (x_ref[0]
          + jnp.dot(a_ref[0], wo_ref[0:da, :], preferred_element_type=F32)
          + jnp.dot(cact[...], wo_ref[da:, :], preferred_element_type=F32))

    hn = _rms(h1, ng_ref[...]).astype(BF16)
    qx = jnp.dot(hn, wq_ref[...], preferred_element_type=F32).astype(BF16)
    hd = d // N_X_HEADS
    heads = []
    for h in range(N_X_HEADS):
        sl = slice(h * hd, (h + 1) * hd)
        s = lax.dot_general(qx[:, sl], mk_ref[0, :, sl], (((1,), (1,)), ((), ())),
                            preferred_element_type=F32)
        s = s - jnp.max(s, axis=-1, keepdims=True)
        e = jnp.exp(s)
        p = e / jnp.sum(e, axis=-1, keepdims=True)
        heads.append(jnp.dot(p.astype(BF16), mv_ref[0, :, sl], preferred_element_type=F32))
    o = jnp.concatenate(heads, axis=-1).astype(BF16)
    o_ref[0] = h1 + jnp.dot(o, wco_ref[...], preferred_element_type=F32)


def _mixcross(x, a, c, cv_w, cv_b, ln_g, ln_b, w_out, ng, w_cq, mk, mv, w_co):
    b, s, d = x.shape
    t = min(TOK_BLOCK, s)
    da, cw, m = a.shape[2], c.shape[2], mk.shape[1]
    hpb = t // CV_HALO
    const = lambda shape: pl.BlockSpec(shape, lambda bi, i: (0,) * len(shape))
    tok = lambda w: pl.BlockSpec((1, t, w), lambda bi, i: (bi, i, 0))
    return pl.pallas_call(
        _mixcross_kernel,
        grid=(b, s // t),
        in_specs=[tok(d), tok(da), tok(cw),
                  pl.BlockSpec((1, CV_HALO, cw), lambda bi, i: (bi, jnp.maximum(i * hpb - 1, 0), 0)),
                  const((CV_KERNEL, cw)), const((1, cw)), const((1, cw)), const((1, cw)),
                  const((da + cw, d)), const((1, d)), const((d, d)),
                  pl.BlockSpec((1, m, d), lambda bi, i: (bi, 0, 0)),
                  pl.BlockSpec((1, m, d), lambda bi, i: (bi, 0, 0)),
                  const((d, d))],
        out_specs=tok(d),
        out_shape=jax.ShapeDtypeStruct((b, s, d), F32),
        scratch_shapes=[pltpu.VMEM((t + CV_HALO, cw), F32), pltpu.VMEM((t, cw), BF16)],
        compiler_params=pltpu.CompilerParams(
            dimension_semantics=("parallel", "parallel"), vmem_limit_bytes=VMEM_LIMIT_BYTES),
        name="mixcross",
    )(x, a, c, c, cv_w, cv_b.reshape(1, cw), ln_g.reshape(1, cw), ln_b.reshape(1, cw),
      w_out.astype(BF16), ng.reshape(1, d), w_cq.astype(BF16), mk, mv, w_co.astype(BF16))


def _convffn_kernel(h_ref, ng_ref, wg_ref, wv_ref, cg_ref, cv_ref, wd_ref, fg_ref, o_ref,
                    hn_scr, acc_scr, gbuf, vbuf, gcarry, vcarry):
    t = h_ref.shape[1]
    i = pl.program_id(1)
    nchunk = wg_ref.shape[0]
    pad = SUBLANES

    @pl.when(i == 0)
    def _():
        gcarry[...] = jnp.zeros(gcarry.shape, F32)
        vcarry[...] = jnp.zeros(vcarry.shape, F32)

    hn_scr[...] = _rms(h_ref[0], ng_ref[...]).astype(BF16)
    acc_scr[...] = jnp.zeros(acc_scr.shape, F32)

    def conv(up, w_ref, f, sbuf, carry):
        sbuf[0:pad, :] = carry[f]
        sbuf[pad:, :] = up
        carry[f] = up[t - pad:, :]
        w = w_ref[f]
        out = w[FFN_KERNEL - 1:FFN_KERNEL, :] * up
        for j in range(FFN_KERNEL - 1):
            off = pad - (FFN_KERNEL - 1) + j
            out = out + w[j:j + 1, :] * sbuf[off:off + t, :]
        return out

    def body(f, carry):
        hn = hn_scr[...]
        g = conv(jnp.dot(hn, wg_ref[f], preferred_element_type=F32), cg_ref, f, gbuf, gcarry)
        v = conv(jnp.dot(hn, wv_ref[f], preferred_element_type=F32), cv_ref, f, vbuf, vcarry)
        z = (g * jax.nn.sigmoid(g) * v).astype(BF16)
        acc_scr[...] += jnp.dot(z, wd_ref[f], preferred_element_type=F32)
        return carry

    lax.fori_loop(0, nchunk, body, 0)
    o_ref[0] = _rms(h_ref[0] + acc_scr[...], fg_ref[...])


def _convffn(h, ng, w_up, dw_w, w_down, fg):
    b, s, d = h.shape
    t = min(TOK_BLOCK, s)
    dff = w_down.shape[0]
    fc = FFN_CHUNK
    nchunk = dff // fc
    assert dff % fc == 0
    wu = w_up.astype(BF16).reshape(d, 2, nchunk, fc).transpose(1, 2, 0, 3)
    dw = dw_w.reshape(FFN_KERNEL, 2, nchunk, fc).transpose(1, 2, 0, 3)
    wd = w_down.astype(BF16).reshape(nchunk, fc, d)
    const = lambda shape: pl.BlockSpec(shape, lambda bi, i: (0,) * len(shape))
    tok = pl.BlockSpec((1, t, d), lambda bi, i: (bi, i, 0))
    return pl.pallas_call(
        _convffn_kernel,
        grid=(b, s // t),
        in_specs=[tok, const((1, d)), const((nchunk, d, fc)), const((nchunk, d, fc)),
                  const((nchunk, FFN_KERNEL, fc)), const((nchunk, FFN_KERNEL, fc)),
                  const((nchunk, fc, d)), const((1, d))],
        out_specs=tok,
        out_shape=jax.ShapeDtypeStruct((b, s, d), F32),
        scratch_shapes=[pltpu.VMEM((t, d), BF16), pltpu.VMEM((t, d), F32),
                        pltpu.VMEM((t + SUBLANES, fc), F32), pltpu.VMEM((t + SUBLANES, fc), F32),
                        pltpu.VMEM((nchunk, SUBLANES, fc), F32),
                        pltpu.VMEM((nchunk, SUBLANES, fc), F32)],
        compiler_params=pltpu.CompilerParams(
            dimension_semantics=("parallel", "arbitrary"), vmem_limit_bytes=VMEM_LIMIT_BYTES),
        name="convffn",
    )(h, ng.reshape(1, d), wu[0], wu[1], dw[0], dw[1], wd, fg.reshape(1, d))


def kernel(x, mem, positions, norm_mix_g, w_in, lam_q1, lam_k1, lam_q2, lam_k2, subln_g, cv_dw_w, cv_dw_b, cv_ln_g, cv_ln_b, w_out, norm_cross_g, norm_mem_g, w_cq, w_ckv, w_co, norm_ffn_g, w_up, ffn_dw_w, w_down, norm_final_g):
    assert w_in.shape[0] == 1, "single-layer operation"
    q, k, v, c = _inproj(x, positions, norm_mix_g[0], w_in[0])
    mk, mv = _memkv(mem, norm_mem_g[0], w_ckv[0])
    a = _diff_attention(q, k, v, lam_q1[0], lam_k1[0], lam_q2[0], lam_k2[0], subln_g[0])
    h2 = _mixcross(x, a, c, cv_dw_w[0], cv_dw_b[0], cv_ln_g[0], cv_ln_b[0], w_out[0],
                   norm_cross_g[0], w_cq[0], mk, mv, w_co[0])
    return _convffn(h2, norm_ffn_g[0], w_up[0], ffn_dw_w[0], w_down[0], norm_final_g)
```

```python
import functools
import math

import jax
import jax.numpy as jnp
from jax import lax
from jax.experimental import pallas as pl
from jax.experimental.pallas import tpu as pltpu

F32 = jnp.float32
BF16 = jnp.bfloat16

N_DA_HEADS = 4
DA_VDIM = 128
DA_QKDIM = 64
ROT_DIM = 16
ROPE_THETA = 500000.0
CV_KERNEL = 31
N_X_HEADS = 4
FFN_KERNEL = 3
EPS = 1e-6
LAM_INIT = 0.8 - 0.6 * math.exp(-0.3 * 0)

LANES = 128
SUBLANES = 8
VMEM_LIMIT_BYTES = 56 * 1024 * 1024

NEG_BIG = -1e30

TOK_BLOCK = 512
ATT_BLOCK = 512
CV_HALO = 32
CV_ROW_TILE = 32
FFN_CHUNK = 256


def _rms(x, g):
    ms = jnp.mean(x * x, axis=-1, keepdims=True)
    return x * lax.rsqrt(ms + EPS) * g


def _inproj_kernel(x_ref, pos_ref, g_ref, w_ref, freq_ref, ma_ref, mb_ref,
                   q_ref, k_ref, v_ref, c_ref):
    da = N_DA_HEADS * DA_VDIM
    xn = _rms(x_ref[0], g_ref[...]).astype(BF16)
    ang = pos_ref[0].astype(F32) * freq_ref[...]
    cos = jnp.cos(ang)
    sin = jnp.sin(ang)
    sa = sin * ma_ref[...]
    sb = sin * mb_ref[...]

    def rope(t):
        return (t * cos + pltpu.roll(t, ROT_DIM // 2, 1) * sa
                + pltpu.roll(t, LANES - ROT_DIM // 2, 1) * sb)

    qp = jnp.dot(xn, w_ref[:, 0:da], preferred_element_type=F32)
    kp = jnp.dot(xn, w_ref[:, da:2 * da], preferred_element_type=F32)
    scale = 1.0 / math.sqrt(DA_QKDIM)
    for h in range(N_DA_HEADS):
        sl = slice(h * LANES, (h + 1) * LANES)
        q_ref[0, :, sl] = (rope(qp[:, sl]) * scale).astype(BF16)
        k_ref[0, :, sl] = rope(kp[:, sl]).astype(BF16)
    v_ref[0] = jnp.dot(xn, w_ref[:, 2 * da:3 * da], preferred_element_type=F32).astype(BF16)
    cw = (w_ref.shape[1] - 3 * da) // 2
    uv = jnp.dot(xn, w_ref[:, 3 * da:3 * da + cw], preferred_element_type=F32)
    ug = jnp.dot(xn, w_ref[:, 3 * da + cw:], preferred_element_type=F32)
    c_ref[0] = uv * jax.nn.sigmoid(ug)


def _inproj(x, positions, g, w_in):
    b, s, d = x.shape
    t = min(TOK_BLOCK, s)
    da = N_DA_HEADS * DA_VDIM
    cw = (w_in.shape[1] - 3 * da) // 2
    inv_freq = ROPE_THETA ** (-jnp.arange(0, ROT_DIM, 2, dtype=F32) / ROT_DIM)
    j = jnp.arange(LANES) % DA_QKDIM
    half = ROT_DIM // 2
    freq = jnp.where(j < ROT_DIM, inv_freq[j % half], 0.0).astype(F32)[None, :]
    ma = jnp.where((j >= half) & (j < ROT_DIM), 1.0, 0.0).astype(F32)[None, :]
    mb = jnp.where(j < half, -1.0, 0.0).astype(F32)[None, :]
    const = lambda shape: pl.BlockSpec(shape, lambda bi, i: (0,) * len(shape))
    tok = lambda w: pl.BlockSpec((1, t, w), lambda bi, i: (bi, i, 0))
    return pl.pallas_call(
        _inproj_kernel,
        grid=(b, s // t),
        in_specs=[tok(d), tok(1), const((1, d)), const(w_in.shape),
                  const((1, LANES)), const((1, LANES)), const((1, LANES))],
        out_specs=[tok(da), tok(da), tok(da), tok(cw)],
        out_shape=[jax.ShapeDtypeStruct((b, s, da), BF16)] * 3
                  + [jax.ShapeDtypeStruct((b, s, cw), F32)],
        compiler_params=pltpu.CompilerParams(
            dimension_semantics=("parallel", "parallel"), vmem_limit_bytes=VMEM_LIMIT_BYTES),
        name="inproj",
    )(x, positions.reshape(b, s, 1), g.reshape(1, d), w_in.astype(BF16), freq, ma, mb)


def _memkv_kernel(m_ref, g_ref, w_ref, k_ref, v_ref):
    d = m_ref.shape[2]
    mn = _rms(m_ref[0], g_ref[...]).astype(BF16)
    kv = jnp.dot(mn, w_ref[...], preferred_element_type=F32)
    scale = 1.0 / math.sqrt(d // N_X_HEADS)
    k_ref[0] = (kv[:, :d] * scale).astype(BF16)
    v_ref[0] = kv[:, d:].astype(BF16)


def _memkv(mem, g, w_ckv):
    b, m, d = mem.shape
    return pl.pallas_call(
        _memkv_kernel,
        grid=(b,),
        in_specs=[pl.BlockSpec((1, m, d), lambda bi: (bi, 0, 0)),
                  pl.BlockSpec((1, d), lambda bi: (0, 0)),
                  pl.BlockSpec((d, 2 * d), lambda bi: (0, 0))],
        out_specs=[pl.BlockSpec((1, m, d), lambda bi: (bi, 0, 0))] * 2,
        out_shape=[jax.ShapeDtypeStruct((b, m, d), BF16)] * 2,
        compiler_params=pltpu.CompilerParams(
            dimension_semantics=("parallel",), vmem_limit_bytes=VMEM_LIMIT_BYTES),
        name="memkv",
    )(mem, g.reshape(1, d), w_ckv.astype(BF16))


def _attn_kernel(q_ref, k_ref, v_ref, lq1_ref, lk1_ref, lq2_ref, lk2_ref, g_ref, o_ref,
                 qs_scr, sa_scr, sb_scr, m_scr, acc_scr):
    t = ATT_BLOCK
    rows = 2 * t
    qi = pl.program_id(2)
    q = q_ref[0]
    lane = lax.broadcasted_iota(jnp.int32, (t, LANES), 1)
    zero = jnp.zeros_like(q)
    qs_scr[0:t, :] = jnp.where(lane < DA_QKDIM, q, zero)
    qs_scr[t:rows, :] = jnp.where(lane >= DA_QKDIM, q, zero)
    ones_col = jnp.where(lane == 0, 1.0, 0.0).astype(BF16)

    m_scr[...] = jnp.full(m_scr.shape, NEG_BIG, F32)
    acc_scr[...] = jnp.zeros(acc_scr.shape, F32)

    def scores(j, dst):
        k0 = pl.multiple_of(j * t, t)
        dst[...] = lax.dot_general(qs_scr[...], k_ref[0, pl.ds(k0, t), :],
                                   (((1,), (1,)), ((), ())), preferred_element_type=F32)

    def consume(j, src, masked):
        k0 = pl.multiple_of(j * t, t)
        vc = jnp.concatenate([v_ref[0, pl.ds(k0, t), :], ones_col], axis=1)
        s = src[...]
        if masked:
            row = lax.broadcasted_iota(jnp.int32, (rows, t), 0) % t
            col = lax.broadcasted_iota(jnp.int32, (rows, t), 1)
            s = jnp.where(col <= row, s, NEG_BIG)
        m_old = m_scr[...]
        m_new = jnp.maximum(m_old, jnp.max(s, axis=1, keepdims=True))
        alpha = jnp.exp(m_old - m_new)
        p = jnp.exp(s - jnp.tile(m_new, (1, t // LANES))).astype(BF16)
        acc_scr[...] = (jnp.tile(alpha, (1, 2)) * acc_scr[...]
                        + jnp.dot(p, vc, preferred_element_type=F32))
        m_scr[...] = m_new

    scores(0, sa_scr)

    def pair(i, carry):
        j = 2 * i
        scores(j + 1, sb_scr)
        consume(j, sa_scr, masked=False)
        scores(j + 2, sa_scr)
        consume(j + 1, sb_scr, masked=False)
        return carry

    lax.fori_loop(0, qi // 2, pair, 0)

    @pl.when(qi % 2 == 1)
    def _():
        scores(qi, sb_scr)
        consume(qi - 1, sa_scr, masked=False)
        consume(qi, sb_scr, masked=True)

    @pl.when(qi % 2 == 0)
    def _():
        consume(qi, sa_scr, masked=True)

    lam = (jnp.exp(jnp.sum(lq1_ref[...] * lk1_ref[...], keepdims=True))
           - jnp.exp(jnp.sum(lq2_ref[...] * lk2_ref[...], keepdims=True)) + LAM_INIT)
    acc = acc_scr[...]
    o = acc[:, :DA_VDIM] / acc[:, DA_VDIM:DA_VDIM + 1]
    a = o[:t] - lam * o[t:]
    o_ref[0] = (_rms(a, g_ref[...]) * (1.0 - LAM_INIT)).astype(o_ref.dtype)


def _diff_attention(q, k, v, lq1, lk1, lq2, lk2, subln_g):
    b, s, _ = q.shape
    tq = ATT_BLOCK
    assert s % tq == 0
    vec = lambda n: pl.BlockSpec((1, n), lambda bi, h, i: (0, 0))
    return pl.pallas_call(
        _attn_kernel,
        grid=(b, N_DA_HEADS, s // tq),
        in_specs=[pl.BlockSpec((1, tq, LANES), lambda bi, h, i: (bi, i, h)),
                  pl.BlockSpec((1, s, LANES), lambda bi, h, i: (bi, 0, h)),
                  pl.BlockSpec((1, s, DA_VDIM), lambda bi, h, i: (bi, 0, h)),
                  vec(DA_QKDIM), vec(DA_QKDIM), vec(DA_QKDIM), vec(DA_QKDIM), vec(DA_VDIM)],
        out_specs=pl.BlockSpec((1, tq, DA_VDIM), lambda bi, h, i: (bi, i, h)),
        out_shape=jax.ShapeDtypeStruct((b, s, N_DA_HEADS * DA_VDIM), BF16),
        scratch_shapes=[pltpu.VMEM((2 * tq, LANES), BF16),
                        pltpu.VMEM((2 * tq, tq), F32), pltpu.VMEM((2 * tq, tq), F32),
                        pltpu.VMEM((2 * tq, LANES), F32),
                        pltpu.VMEM((2 * tq, 2 * DA_VDIM), F32)],
        compiler_params=pltpu.CompilerParams(
            dimension_semantics=("parallel", "parallel", "arbitrary"),
            vmem_limit_bytes=VMEM_LIMIT_BYTES),
        name="diffattn",
    )(q, k, v, lq1.reshape(1, -1), lk1.reshape(1, -1), lq2.reshape(1, -1), lk2.reshape(1, -1),
      subln_g.reshape(1, -1))


def _mixcross_kernel(x_ref, a_ref, c_ref, halo_ref, cw_ref, cb_ref, lg_ref, lb_ref, wo_ref,
                     ng_ref, wq_ref, mk_ref, mv_ref, wco_ref, o_ref, buf, cact):
    t = x_ref.shape[1]
    d = x_ref.shape[2]
    i = pl.program_id(1)
    halo = halo_ref[0]
    buf[0:CV_HALO, :] = jnp.where(i > 0, halo, jnp.zeros_like(halo))
    buf[CV_HALO:, :] = c_ref[0]

    base = CV_HALO - (CV_KERNEL - 1)
    for r0 in range(0, t, CV_ROW_TILE):
        acc = jnp.zeros((CV_ROW_TILE, buf.shape[1]), F32) + cb_ref[...]
        for j in range(CV_KERNEL):
            acc = acc + cw_ref[j:j + 1, :] * buf[r0 + base + j:r0 + base + j + CV_ROW_TILE, :]
        mu = jnp.mean(acc, axis=-1, keepdims=True)
        xc = acc - mu
        var = jnp.mean(xc * xc, axis=-1, keepdims=True)
        y = xc * lax.rsqrt(var + EPS) * lg_ref[...] + lb_ref[...]
        cact[r0:r0 + CV_ROW_TILE, :] = (y * jax.nn.sigmoid(y)).astype(BF16)

    da = a_ref.shape[2]
    h1 = (x_ref[0]
          + jnp.dot(a_ref[0], wo_ref[0:da, :], preferred_element_type=F32)
          + jnp.dot(cact[...], wo_ref[da:, :], preferred_element_type=F32))

    hn = _rms(h1, ng_ref[...]).astype(BF16)
    qx = jnp.dot(hn, wq_ref[...], preferred_element_type=F32).astype(BF16)
    hd = d // N_X_HEADS
    heads = []
    for h in range(N_X_HEADS):
        sl = slice(h * hd, (h + 1) * hd)
        s = lax.dot_general(qx[:, sl], mk_ref[0, :, sl], (((1,), (1,)), ((), ())),
                            preferred_element_type=F32)
        s = s - jnp.max(s, axis=-1, keepdims=True)
        e = jnp.exp(s)
        p = e / jnp.sum(e, axis=-1, keepdims=True)
        heads.append(jnp.dot(p.astype(BF16), mv_ref[0, :, sl], preferred_element_type=F32))
    o = jnp.concatenate(heads, axis=-1).astype(BF16)
    o_ref[0] = h1 + jnp.dot(o, wco_ref[...], preferred_element_type=F32)


def _mixcross(x, a, c, cv_w, cv_b, ln_g, ln_b, w_out, ng, w_cq, mk, mv, w_co):
    b, s, d = x.shape
    t = min(TOK_BLOCK, s)
    da, cw, m = a.shape[2], c.shape[2], mk.shape[1]
    hpb = t // CV_HALO
    const = lambda shape: pl.BlockSpec(shape, lambda bi, i: (0,) * len(shape))
    tok = lambda w: pl.BlockSpec((1, t, w), lambda bi, i: (bi, i, 0))
    return pl.pallas_call(
        _mixcross_kernel,
        grid=(b, s // t),
        in_specs=[tok(d), tok(da), tok(cw),
                  pl.BlockSpec((1, CV_HALO, cw), lambda bi, i: (bi, jnp.maximum(i * hpb - 1, 0), 0)),
                  const((CV_KERNEL, cw)), const((1, cw)), const((1, cw)), const((1, cw)),
                  const((da + cw, d)), const((1, d)), const((d, d)),
                  pl.BlockSpec((1, m, d), lambda bi, i: (bi, 0, 0)),
                  pl.BlockSpec((1, m, d), lambda bi, i: (bi, 0, 0)),
                  const((d, d))],
        out_specs=tok(d),
        out_shape=jax.ShapeDtypeStruct((b, s, d), F32),
        scratch_shapes=[pltpu.VMEM((t + CV_HALO, cw), F32), pltpu.VMEM((t, cw), BF16)],
        compiler_params=pltpu.CompilerParams(
            dimension_semantics=("parallel", "parallel"), vmem_limit_bytes=VMEM_LIMIT_BYTES),
        name="mixcross",
    )(x, a, c, c, cv_w, cv_b.reshape(1, cw), ln_g.reshape(1, cw), ln_b.reshape(1, cw),
      w_out.astype(BF16), ng.reshape(1, d), w_cq.astype(BF16), mk, mv, w_co.astype(BF16))


def _convffn_kernel(h_ref, ng_ref, wg_ref, wv_ref, cg_ref, cv_ref, wd_ref, fg_ref, o_ref,
                    hn_scr, acc_scr, gbuf, vbuf, gcarry, vcarry):
    t = h_ref.shape[1]
    i = pl.program_id(1)
    nchunk = wg_ref.shape[0]
    pad = SUBLANES

    @pl.when(i == 0)
    def _():
        gcarry[...] = jnp.zeros(gcarry.shape, F32)
        vcarry[...] = jnp.zeros(vcarry.shape, F32)

    hn_scr[...] = _rms(h_ref[0], ng_ref[...]).astype(BF16)
    acc_scr[...] = jnp.zeros(acc_scr.shape, F32)

    def conv(up, w_ref, f, sbuf, carry):
        sbuf[0:pad, :] = carry[f]
        sbuf[pad:, :] = up
        carry[f] = up[t - pad:, :]
        w = w_ref[f]
        out = w[FFN_KERNEL - 1:FFN_KERNEL, :] * up
        for j in range(FFN_KERNEL - 1):
            off = pad - (FFN_KERNEL - 1) + j
            out = out + w[j:j + 1, :] * sbuf[off:off + t, :]
        return out

    def body(f, carry):
        hn = hn_scr[...]
        g = conv(jnp.dot(hn, wg_ref[f], preferred_element_type=F32), cg_ref, f, gbuf, gcarry)
        v = conv(jnp.dot(hn, wv_ref[f], preferred_element_type=F32), cv_ref, f, vbuf, vcarry)
        z = (g * jax.nn.sigmoid(g) * v).astype(BF16)
        acc_scr[...] += jnp.dot(z, wd_ref[f], preferred_element_type=F32)
        return carry

    lax.fori_loop(0, nchunk, body, 0)
    o_ref[0] = _rms(h_ref[0] + acc_scr[...], fg_ref[...])


def _convffn(h, ng, w_up, dw_w, w_down, fg):
    b, s, d = h.shape
    t = min(TOK_BLOCK, s)
    dff = w_down.shape[0]
    fc = FFN_CHUNK
    nchunk = dff // fc
    assert dff % fc == 0
    wu = w_up.astype(BF16).reshape(d, 2, nchunk, fc).transpose(1, 2, 0, 3)
    dw = dw_w.reshape(FFN_KERNEL, 2, nchunk, fc).transpose(1, 2, 0, 3)
    wd = w_down.astype(BF16).reshape(nchunk, fc, d)
    const = lambda shape: pl.BlockSpec(shape, lambda bi, i: (0,) * len(shape))
    tok = pl.BlockSpec((1, t, d), lambda bi, i: (bi, i, 0))
    return pl.pallas_call(
        _convffn_kernel,
        grid=(b, s // t),
        in_specs=[tok, const((1, d)), const((nchunk, d, fc)), const((nchunk, d, fc)),
                  const((nchunk, FFN_KERNEL, fc)), const((nchunk, FFN_KERNEL, fc)),
                  const((nchunk, fc, d)), const((1, d))],
        out_specs=tok,
        out_shape=jax.ShapeDtypeStruct((b, s, d), F32),
        scratch_shapes=[pltpu.VMEM((t, d), BF16), pltpu.VMEM((t, d), F32),
                        pltpu.VMEM((t + SUBLANES, fc), F32), pltpu.VMEM((t + SUBLANES, fc), F32),
                        pltpu.VMEM((nchunk, SUBLANES, fc), F32),
                        pltpu.VMEM((nchunk, SUBLANES, fc), F32)],
        compiler_params=pltpu.CompilerParams(
            dimension_semantics=("parallel", "arbitrary"), vmem_limit_bytes=VMEM_LIMIT_BYTES),
        name="convffn",
    )(h, ng.reshape(1, d), wu[0], wu[1], dw[0], dw[1], wd, fg.reshape(1, d))


def kernel(x, mem, positions, norm_mix_g, w_in, lam_q1, lam_k1, lam_q2, lam_k2, subln_g, cv_dw_w, cv_dw_b, cv_ln_g, cv_ln_b, w_out, norm_cross_g, norm_mem_g, w_cq, w_ckv, w_co, norm_ffn_g, w_up, ffn_dw_w, w_down, norm_final_g):
    assert w_in.shape[0] == 1, "single-layer operation"
    q, k, v, c = _inproj(x, positions, norm_mix_g[0], w_in[0])
    mk, mv = _memkv(mem, norm_mem_g[0], w_ckv[0])
    a = _diff_attention(q, k, v, lam_q1[0], lam_k1[0], lam_q2[0], lam_k2[0], subln_g[0])
    h2 = _mixcross(x, a, c, cv_dw_w[0], cv_dw_b[0], cv_ln_g[0], cv_ln_b[0], w_out[0],
                   norm_cross_g[0], w_cq[0], mk, mv, w_co[0])
    return _convffn(h2, norm_ffn_g[0], w_up[0], ffn_dw_w[0], w_down[0], norm_final_g)
```

```python
import functools
import math

import jax
import jax.numpy as jnp
from jax import lax
from jax.experimental import pallas as pl
from jax.experimental.pallas import tpu as pltpu

F32 = jnp.float32
BF16 = jnp.bfloat16

N_DA_HEADS = 4
DA_VDIM = 128
DA_QKDIM = 64
ROT_DIM = 16
ROPE_THETA = 500000.0
CV_KERNEL = 31
N_X_HEADS = 4
FFN_KERNEL = 3
EPS = 1e-6
LAM_INIT = 0.8 - 0.6 * math.exp(-0.3 * 0)

LANES = 128
SUBLANES = 8
VMEM_LIMIT_BYTES = 56 * 1024 * 1024

NEG_BIG = -1e30

TOK_BLOCK = 512
ATT_BLOCK = 512
CV_HALO = 32
CV_ROW_TILE = 32
FFN_CHUNK = 256


def _rms(x, g):
    ms = jnp.mean(x * x, axis=-1, keepdims=True)
    return x * lax.rsqrt(ms + EPS) * g


def _inproj_kernel(x_ref, pos_ref, g_ref, w_ref, freq_ref, ma_ref, mb_ref,
                   q_ref, k_ref, v_ref, c_ref):
    da = N_DA_HEADS * DA_VDIM
    xn = _rms(x_ref[0], g_ref[...]).astype(BF16)
    ang = pos_ref[0].astype(F32) * freq_ref[...]
    cos = jnp.cos(ang)
    sin = jnp.sin(ang)
    sa = sin * ma_ref[...]
    sb = sin * mb_ref[...]

    def rope(t):
        return (t * cos + pltpu.roll(t, ROT_DIM // 2, 1) * sa
                + pltpu.roll(t, LANES - ROT_DIM // 2, 1) * sb)

    qp = jnp.dot(xn, w_ref[:, 0:da], preferred_element_type=F32)
    kp = jnp.dot(xn, w_ref[:, da:2 * da], preferred_element_type=F32)
    scale = 1.0 / math.sqrt(DA_QKDIM)
    for h in range(N_DA_HEADS):
        sl = slice(h * LANES, (h + 1) * LANES)
        q_ref[0, :, sl] = (rope(qp[:, sl]) * scale).astype(BF16)
        k_ref[0, :, sl] = rope(kp[:, sl]).astype(BF16)
    v_ref[0] = jnp.dot(xn, w_ref[:, 2 * da:3 * da], preferred_element_type=F32).astype(BF16)
    cw = (w_ref.shape[1] - 3 * da) // 2
    uv = jnp.dot(xn, w_ref[:, 3 * da:3 * da + cw], preferred_element_type=F32)
    ug = jnp.dot(xn, w_ref[:, 3 * da + cw:], preferred_element_type=F32)
    c_ref[0] = uv * jax.nn.sigmoid(ug)


def _inproj(x, positions, g, w_in):
    b, s, d = x.shape
    t = min(TOK_BLOCK, s)
    da = N_DA_HEADS * DA_VDIM
    cw = (w_in.shape[1] - 3 * da) // 2
    inv_freq = ROPE_THETA ** (-jnp.arange(0, ROT_DIM, 2, dtype=F32) / ROT_DIM)
    j = jnp.arange(LANES) % DA_QKDIM
    half = ROT_DIM // 2
    freq = jnp.where(j < ROT_DIM, inv_freq[j % half], 0.0).astype(F32)[None, :]
    ma = jnp.where((j >= half) & (j < ROT_DIM), 1.0, 0.0).astype(F32)[None, :]
    mb = jnp.where(j < half, -1.0, 0.0).astype(F32)[None, :]
    const = lambda shape: pl.BlockSpec(shape, lambda bi, i: (0,) * len(shape))
    tok = lambda w: pl.BlockSpec((1, t, w), lambda bi, i: (bi, i, 0))
    return pl.pallas_call(
        _inproj_kernel,
        grid=(b, s // t),
        in_specs=[tok(d), tok(1), const((1, d)), const(w_in.shape),
                  const((1, LANES)), const((1, LANES)), const((1, LANES))],
        out_specs=[tok(da), tok(da), tok(da), tok(cw)],
        out_shape=[jax.ShapeDtypeStruct((b, s, da), BF16)] * 3
                  + [jax.ShapeDtypeStruct((b, s, cw), F32)],
        compiler_params=pltpu.CompilerParams(
            dimension_semantics=("parallel", "parallel"), vmem_limit_bytes=VMEM_LIMIT_BYTES),
        name="inproj",
    )(x, positions.reshape(b, s, 1), g.reshape(1, d), w_in.astype(BF16), freq, ma, mb)


def _memkv_kernel(m_ref, g_ref, w_ref, k_ref, v_ref):
    d = m_ref.shape[2]
    mn = _rms(m_ref[0], g_ref[...]).astype(BF16)
    kv = jnp.dot(mn, w_ref[...], preferred_element_type=F32)
    scale = 1.0 / math.sqrt(d // N_X_HEADS)
    k_ref[0] = (kv[:, :d] * scale).astype(BF16)
    v_ref[0] = kv[:, d:].astype(BF16)


def _memkv(mem, g, w_ckv):
    b, m, d = mem.shape
    return pl.pallas_call(
        _memkv_kernel,
        grid=(b,),
        in_specs=[pl.BlockSpec((1, m, d), lambda bi: (bi, 0, 0)),
                  pl.BlockSpec((1, d), lambda bi: (0, 0)),
                  pl.BlockSpec((d, 2 * d), lambda bi: (0, 0))],
        out_specs=[pl.BlockSpec((1, m, d), lambda bi: (bi, 0, 0))] * 2,
        out_shape=[jax.ShapeDtypeStruct((b, m, d), BF16)] * 2,
        compiler_params=pltpu.CompilerParams(
            dimension_semantics=("parallel",), vmem_limit_bytes=VMEM_LIMIT_BYTES),
        name="memkv",
    )(mem, g.reshape(1, d), w_ckv.astype(BF16))


def _attn_kernel(q_ref, k_ref, v_ref, lq1_ref, lk1_ref, lq2_ref, lk2_ref, g_ref, o_ref,
                 qs_scr, sa_scr, sb_scr, m_scr, acc_scr):
    t = ATT_BLOCK
    rows = 2 * t
    qi = pl.program_id(2)
    q = q_ref[0]
    lane = lax.broadcasted_iota(jnp.int32, (t, LANES), 1)
    zero = jnp.zeros_like(q)
    qs_scr[0:t, :] = jnp.where(lane < DA_QKDIM, q, zero)
    qs_scr[t:rows, :] = jnp.where(lane >= DA_QKDIM, q, zero)
    ones_col = jnp.where(lane == 0, 1.0, 0.0).astype(BF16)

    m_scr[...] = jnp.full(m_scr.shape, NEG_BIG, F32)
    acc_scr[...] = jnp.zeros(acc_scr.shape, F32)

    def scores(j, dst):
        k0 = pl.multiple_of(j * t, t)
        dst[...] = lax.dot_general(qs_scr[...], k_ref[0, pl.ds(k0, t), :],
                                   (((1,), (1,)), ((), ())), preferred_element_type=F32)

    def consume(j, src, masked):
        k0 = pl.multiple_of(j * t, t)
        vc = jnp.concatenate([v_ref[0, pl.ds(k0, t), :], ones_col], axis=1)
        s = src[...]
        if masked:
            row = lax.broadcasted_iota(jnp.int32, (rows, t), 0) % t
            col = lax.broadcasted_iota(jnp.int32, (rows, t), 1)
            s = jnp.where(col <= row, s, NEG_BIG)
        m_old = m_scr[...]
        m_new = jnp.maximum(m_old, jnp.max(s, axis=1, keepdims=True))
        alpha = jnp.exp(m_old - m_new)
        p = jnp.exp(s - jnp.tile(m_new, (1, t // LANES))).astype(BF16)
        acc_scr[...] = (jnp.tile(alpha, (1, 2)) * acc_scr[...]
                        + jnp.dot(p, vc, preferred_element_type=F32))
        m_scr[...] = m_new

    scores(0, sa_scr)

    def pair(i, carry):
        j = 2 * i
        scores(j + 1, sb_scr)
        consume(j, sa_scr, masked=False)
        scores(j + 2, sa_scr)
        consume(j + 1, sb_scr, masked=False)
        return carry

    lax.fori_loop(0, qi // 2, pair, 0)

    @pl.when(qi % 2 == 1)
    def _():
        scores(qi, sb_scr)
        consume(qi - 1, sa_scr, masked=False)
        consume(qi, sb_scr, masked=True)

    @pl.when(qi % 2 == 0)
    def _():
        consume(qi, sa_scr, masked=True)

    lam = (jnp.exp(jnp.sum(lq1_ref[...] * lk1_ref[...], keepdims=True))
           - jnp.exp(jnp.sum(lq2_ref[...] * lk2_ref[...], keepdims=True)) + LAM_INIT)
    acc = acc_scr[...]
    o = acc[:, :DA_VDIM] / acc[:, DA_VDIM:DA_VDIM + 1]
    a = o[:t] - lam * o[t:]
    o_ref[0] = (_rms(a, g_ref[...]) * (1.0 - LAM_INIT)).astype(o_ref.dtype)


def _diff_attention(q, k, v, lq1, lk1, lq2, lk2, subln_g):
    b, s, _ = q.shape
    tq = ATT_BLOCK
    assert s % tq == 0
    vec = lambda n: pl.BlockSpec((1, n), lambda bi, h, i: (0, 0))
    return pl.pallas_call(
        _attn_kernel,
        grid=(b, N_DA_HEADS, s // tq),
        in_specs=[pl.BlockSpec((1, tq, LANES), lambda bi, h, i: (bi, i, h)),
                  pl.BlockSpec((1, s, LANES), lambda bi, h, i: (bi, 0, h)),
                  pl.BlockSpec((1, s, DA_VDIM), lambda bi, h, i: (bi, 0, h)),
                  vec(DA_QKDIM), vec(DA_QKDIM), vec(DA_QKDIM), vec(DA_QKDIM), vec(DA_VDIM)],
        out_specs=pl.BlockSpec((1, tq, DA_VDIM), lambda bi, h, i: (bi, i, h)),
        out_shape=jax.ShapeDtypeStruct((b, s, N_DA_HEADS * DA_VDIM), BF16),
        scratch_shapes=[pltpu.VMEM((2 * tq, LANES), BF16),
                        pltpu.VMEM((2 * tq, tq), F32), pltpu.VMEM((2 * tq, tq), F32),
                        pltpu.VMEM((2 * tq, LANES), F32),
                        pltpu.VMEM((2 * tq, 2 * DA_VDIM), F32)],
        compiler_params=pltpu.CompilerParams(
            dimension_semantics=("parallel", "parallel", "arbitrary"),
            vmem_limit_bytes=VMEM_LIMIT_BYTES),
        name="diffattn",
    )(q, k, v, lq1.reshape(1, -1), lk1.reshape(1, -1), lq2.reshape(1, -1), lk2.reshape(1, -1),
      subln_g.reshape(1, -1))


def _mixcross_kernel(x_ref, a_ref, c_ref, halo_ref, cw_ref, cb_ref, lg_ref, lb_ref, wo_ref,
                     ng_ref, wq_ref, mk_ref, mv_ref, wco_ref, o_ref, buf, shf, cact):
    t = x_ref.shape[1]
    d = x_ref.shape[2]
    i = pl.program_id(1)
    n = t + CV_HALO
    halo = halo_ref[0]
    buf[0:CV_HALO, :] = jnp.where(i > 0, halo, jnp.zeros_like(halo))
    buf[CV_HALO:, :] = c_ref[0]
    for p in range(1, SUBLANES):
        shf[p - 1, SUBLANES:n, :] = buf[SUBLANES - p:n - p, :]

    for r0 in range(0, t, CV_ROW_TILE):
        acc = jnp.zeros((CV_ROW_TILE, buf.shape[1]), F32) + cb_ref[...]
        for dl in range(CV_KERNEL):
            al, p = divmod(dl, SUBLANES)
            row = r0 + CV_HALO - al * SUBLANES
            src = buf if p == 0 else shf.at[p - 1]
            wj = CV_KERNEL - 1 - dl
            wt = jnp.tile(cw_ref[wj], (CV_ROW_TILE // SUBLANES, 1))
            acc = acc + wt * src[row:row + CV_ROW_TILE, :]
        mu = jnp.mean(acc, axis=-1, keepdims=True)
        xc = acc - mu
        var = jnp.mean(xc * xc, axis=-1, keepdims=True)
        y = xc * lax.rsqrt(var + EPS) * lg_ref[...] + lb_ref[...]
        cact[r0:r0 + CV_ROW_TILE, :] = (y * jax.nn.sigmoid(y)).astype(BF16)

    da = a_ref.shape[2]
    h1 = (x_ref[0]
          + jnp.dot(a_ref[0], wo_ref[0:da, :], preferred_element_type=F32)
          + jnp.dot(cact[...], wo_ref[da:, :], preferred_element_type=F32))

    hn = _rms(h1, ng_ref[...]).astype(BF16)
    qx = jnp.dot(hn, wq_ref[...], preferred_element_type=F32).astype(BF16)
    hd = d // N_X_HEADS
    heads = []
    for h in range(N_X_HEADS):
        sl = slice(h * hd, (h + 1) * hd)
        s = lax.dot_general(qx[:, sl], mk_ref[0, :, sl], (((1,), (1,)), ((), ())),
                            preferred_element_type=F32)
        s = s - jnp.max(s, axis=-1, keepdims=True)
        e = jnp.exp(s)
        p = e / jnp.sum(e, axis=-1, keepdims=True)
        heads.append(jnp.dot(p.astype(BF16), mv_ref[0, :, sl], preferred_element_type=F32))
    o = jnp.concatenate(heads, axis=-1).astype(BF16)
    o_ref[0] = h1 + jnp.dot(o, wco_ref[...], preferred_element_type=F32)


def _mixcross(x, a, c, cv_w, cv_b, ln_g, ln_b, w_out, ng, w_cq, mk, mv, w_co):
    b, s, d = x.shape
    t = min(TOK_BLOCK, s)
    da, cw, m = a.shape[2], c.shape[2], mk.shape[1]
    hpb = t // CV_HALO
    const = lambda shape: pl.BlockSpec(shape, lambda bi, i: (0,) * len(shape))
    tok = lambda w: pl.BlockSpec((1, t, w), lambda bi, i: (bi, i, 0))
    return pl.pallas_call(
        _mixcross_kernel,
        grid=(b, s // t),
        in_specs=[tok(d), tok(da), tok(cw),
                  pl.BlockSpec((1, CV_HALO, cw), lambda bi, i: (bi, jnp.maximum(i * hpb - 1, 0), 0)),
                  const((CV_KERNEL, SUBLANES, cw)), const((1, cw)), const((1, cw)), const((1, cw)),
                  const((da + cw, d)), const((1, d)), const((d, d)),
                  pl.BlockSpec((1, m, d), lambda bi, i: (bi, 0, 0)),
                  pl.BlockSpec((1, m, d), lambda bi, i: (bi, 0, 0)),
                  const((d, d))],
        out_specs=tok(d),
        out_shape=jax.ShapeDtypeStruct((b, s, d), F32),
        scratch_shapes=[pltpu.VMEM((t + CV_HALO, cw), F32),
                        pltpu.VMEM((SUBLANES - 1, t + CV_HALO, cw), F32),
                        pltpu.VMEM((t, cw), BF16)],
        compiler_params=pltpu.CompilerParams(
            dimension_semantics=("parallel", "parallel"), vmem_limit_bytes=VMEM_LIMIT_BYTES),
        name="mixcross",
    )(x, a, c, c, jnp.broadcast_to(cv_w[:, None, :], (CV_KERNEL, SUBLANES, cw)),
      cv_b.reshape(1, cw), ln_g.reshape(1, cw), ln_b.reshape(1, cw),
      w_out.astype(BF16), ng.reshape(1, d), w_cq.astype(BF16), mk, mv, w_co.astype(BF16))


def _convffn_kernel(h_ref, ng_ref, wg_ref, wv_ref, cg_ref, cv_ref, wd_ref, fg_ref, o_ref,
                    hn_scr, acc_scr, ga, va, gb, vb, gcarry, vcarry):
    t = h_ref.shape[1]
    i = pl.program_id(1)
    nchunk = wg_ref.shape[0]
    pad = SUBLANES

    @pl.when(i == 0)
    def _():
        gcarry[...] = jnp.zeros(gcarry.shape, F32)
        vcarry[...] = jnp.zeros(vcarry.shape, F32)

    hn_scr[...] = _rms(h_ref[0], ng_ref[...]).astype(BF16)
    acc_scr[...] = jnp.zeros(acc_scr.shape, F32)

    def up(f, gdst, vdst):
        hn = hn_scr[...]
        gdst[pad:, :] = jnp.dot(hn, wg_ref[f], preferred_element_type=F32)
        vdst[pad:, :] = jnp.dot(hn, wv_ref[f], preferred_element_type=F32)

    def conv(sbuf, w_ref, f, carry):
        sbuf[0:pad, :] = carry[f]
        carry[f] = sbuf[t:t + pad, :]
        w = w_ref[f]
        out = w[FFN_KERNEL - 1:FFN_KERNEL, :] * sbuf[pad:, :]
        for j in range(FFN_KERNEL - 1):
            off = pad - (FFN_KERNEL - 1) + j
            out = out + w[j:j + 1, :] * sbuf[off:off + t, :]
        return out

    def down(f, gsrc, vsrc):
        g = conv(gsrc, cg_ref, f, gcarry)
        v = conv(vsrc, cv_ref, f, vcarry)
        z = (g * jax.nn.sigmoid(g) * v).astype(BF16)
        acc_scr[...] += jnp.dot(z, wd_ref[f], preferred_element_type=F32)

    up(0, ga, va)

    def pair(k, carry):
        f = 2 * k
        up(f + 1, gb, vb)
        down(f, ga, va)
        up(f + 2, ga, va)
        down(f + 1, gb, vb)
        return carry

    lax.fori_loop(0, nchunk // 2, pair, 0)
    down(nchunk - 1, ga, va)
    o_ref[0] = _rms(h_ref[0] + acc_scr[...], fg_ref[...])


def _convffn(h, ng, w_up, dw_w, w_down, fg):
    b, s, d = h.shape
    t = min(TOK_BLOCK, s)
    dff = w_down.shape[0]
    fc = FFN_CHUNK
    nchunk = dff // fc
    assert dff % fc == 0 and nchunk % 2 == 1
    wu = w_up.astype(BF16).reshape(d, 2, nchunk, fc).transpose(1, 2, 0, 3)
    dw = dw_w.reshape(FFN_KERNEL, 2, nchunk, fc).transpose(1, 2, 0, 3)
    wd = w_down.astype(BF16).reshape(nchunk, fc, d)
    const = lambda shape: pl.BlockSpec(shape, lambda bi, i: (0,) * len(shape))
    tok = pl.BlockSpec((1, t, d), lambda bi, i: (bi, i, 0))
    return pl.pallas_call(
        _convffn_kernel,
        grid=(b, s // t),
        in_specs=[tok, const((1, d)), const((nchunk, d, fc)), const((nchunk, d, fc)),
                  const((nchunk, FFN_KERNEL, fc)), const((nchunk, FFN_KERNEL, fc)),
                  const((nchunk, fc, d)), const((1, d))],
        out_specs=tok,
        out_shape=jax.ShapeDtypeStruct((b, s, d), F32),
        scratch_shapes=[pltpu.VMEM((t, d), BF16), pltpu.VMEM((t, d), F32),
                        pltpu.VMEM((t + SUBLANES, fc), F32), pltpu.VMEM((t + SUBLANES, fc), F32),
                        pltpu.VMEM((t + SUBLANES, fc), F32), pltpu.VMEM((t + SUBLANES, fc), F32),
                        pltpu.VMEM((nchunk, SUBLANES, fc), F32),
                        pltpu.VMEM((nchunk, SUBLANES, fc), F32)],
        compiler_params=pltpu.CompilerParams(
            dimension_semantics=("parallel", "arbitrary"), vmem_limit_bytes=VMEM_LIMIT_BYTES),
        name="convffn",
    )(h, ng.reshape(1, d), wu[0], wu[1], dw[0], dw[1], wd, fg.reshape(1, d))


def kernel(x, mem, positions, norm_mix_g, w_in, lam_q1, lam_k1, lam_q2, lam_k2, subln_g, cv_dw_w, cv_dw_b, cv_ln_g, cv_ln_b, w_out, norm_cross_g, norm_mem_g, w_cq, w_ckv, w_co, norm_ffn_g, w_up, ffn_dw_w, w_down, norm_final_g):
    assert w_in.shape[0] == 1, "single-layer operation"
    q, k, v, c = _inproj(x, positions, norm_mix_g[0], w_in[0])
    mk, mv = _memkv(mem, norm_mem_g[0], w_ckv[0])
    a = _diff_attention(q, k, v, lam_q1[0], lam_k1[0], lam_q2[0], lam_k2[0], subln_g[0])
    h2 = _mixcross(x, a, c, cv_dw_w[0], cv_dw_b[0], cv_ln_g[0], cv_ln_b[0], w_out[0],
                   norm_cross_g[0], w_cq[0], mk, mv, w_co[0])
    return _convffn(h2, norm_ffn_g[0], w_up[0], ffn_dw_w[0], w_down[0], norm_final_g)
```

```python
import functools
import math

import jax
import jax.numpy as jnp
from jax import lax
from jax.experimental import pallas as pl
from jax.experimental.pallas import tpu as pltpu

F32 = jnp.float32
BF16 = jnp.bfloat16

N_DA_HEADS = 4
DA_VDIM = 128
DA_QKDIM = 64
ROT_DIM = 16
ROPE_THETA = 500000.0
CV_KERNEL = 31
N_X_HEADS = 4
FFN_KERNEL = 3
EPS = 1e-6
LAM_INIT = 0.8 - 0.6 * math.exp(-0.3 * 0)

LANES = 128
SUBLANES = 8
VMEM_LIMIT_BYTES = 56 * 1024 * 1024

NEG_BIG = -1e30

TOK_BLOCK = 512
ATT_UNROLL = 4
ATT_BLOCK = 512
CV_HALO = 32
CV_ROW_TILE = 32
FFN_CHUNK = 256


def _rms(x, g):
    ms = jnp.mean(x * x, axis=-1, keepdims=True)
    return x * lax.rsqrt(ms + EPS) * g


def _inproj_kernel(x_ref, pos_ref, g_ref, w_ref, freq_ref, ma_ref, mb_ref,
                   q_ref, k_ref, v_ref, c_ref):
    da = N_DA_HEADS * DA_VDIM
    xn = _rms(x_ref[0], g_ref[...]).astype(BF16)
    ang = pos_ref[0].astype(F32) * freq_ref[...]
    cos = jnp.cos(ang)
    sin = jnp.sin(ang)
    sa = sin * ma_ref[...]
    sb = sin * mb_ref[...]

    def rope(t):
        return (t * cos + pltpu.roll(t, ROT_DIM // 2, 1) * sa
                + pltpu.roll(t, LANES - ROT_DIM // 2, 1) * sb)

    qp = jnp.dot(xn, w_ref[:, 0:da], preferred_element_type=F32)
    kp = jnp.dot(xn, w_ref[:, da:2 * da], preferred_element_type=F32)
    scale = math.log2(math.e) / math.sqrt(DA_QKDIM)
    for h in range(N_DA_HEADS):
        sl = slice(h * LANES, (h + 1) * LANES)
        q_ref[0, :, sl] = (rope(qp[:, sl]) * scale).astype(BF16)
        k_ref[0, :, sl] = rope(kp[:, sl]).astype(BF16)
    v_ref[0] = jnp.dot(xn, w_ref[:, 2 * da:3 * da], preferred_element_type=F32).astype(BF16)
    cw = (w_ref.shape[1] - 3 * da) // 2
    uv = jnp.dot(xn, w_ref[:, 3 * da:3 * da + cw], preferred_element_type=F32)
    ug = jnp.dot(xn, w_ref[:, 3 * da + cw:], preferred_element_type=F32)
    c_ref[0] = uv * jax.nn.sigmoid(ug)


def _inproj(x, positions, g, w_in):
    b, s, d = x.shape
    t = min(TOK_BLOCK, s)
    da = N_DA_HEADS * DA_VDIM
    cw = (w_in.shape[1] - 3 * da) // 2
    inv_freq = ROPE_THETA ** (-jnp.arange(0, ROT_DIM, 2, dtype=F32) / ROT_DIM)
    j = jnp.arange(LANES) % DA_QKDIM
    half = ROT_DIM // 2
    freq = jnp.where(j < ROT_DIM, inv_freq[j % half], 0.0).astype(F32)[None, :]
    ma = jnp.where((j >= half) & (j < ROT_DIM), 1.0, 0.0).astype(F32)[None, :]
    mb = jnp.where(j < half, -1.0, 0.0).astype(F32)[None, :]
    const = lambda shape: pl.BlockSpec(shape, lambda bi, i: (0,) * len(shape))
    tok = lambda w: pl.BlockSpec((1, t, w), lambda bi, i: (bi, i, 0))
    return pl.pallas_call(
        _inproj_kernel,
        grid=(b, s // t),
        in_specs=[tok(d), tok(1), const((1, d)), const(w_in.shape),
                  const((1, LANES)), const((1, LANES)), const((1, LANES))],
        out_specs=[tok(da), tok(da), tok(da), tok(cw)],
        out_shape=[jax.ShapeDtypeStruct((b, s, da), BF16)] * 3
                  + [jax.ShapeDtypeStruct((b, s, cw), F32)],
        compiler_params=pltpu.CompilerParams(
            dimension_semantics=("parallel", "parallel"), vmem_limit_bytes=VMEM_LIMIT_BYTES),
        name="inproj",
    )(x, positions.reshape(b, s, 1), g.reshape(1, d), w_in.astype(BF16), freq, ma, mb)


def _memkv_kernel(m_ref, g_ref, w_ref, k_ref, v_ref):
    d = m_ref.shape[2]
    mn = _rms(m_ref[0], g_ref[...]).astype(BF16)
    kv = jnp.dot(mn, w_ref[...], preferred_element_type=F32)
    scale = 1.0 / math.sqrt(d // N_X_HEADS)
    k_ref[0] = (kv[:, :d] * scale).astype(BF16)
    v_ref[0] = kv[:, d:].astype(BF16)


def _memkv(mem, g, w_ckv):
    b, m, d = mem.shape
    return pl.pallas_call(
        _memkv_kernel,
        grid=(b,),
        in_specs=[pl.BlockSpec((1, m, d), lambda bi: (bi, 0, 0)),
                  pl.BlockSpec((1, d), lambda bi: (0, 0)),
                  pl.BlockSpec((d, 2 * d), lambda bi: (0, 0))],
        out_specs=[pl.BlockSpec((1, m, d), lambda bi: (bi, 0, 0))] * 2,
        out_shape=[jax.ShapeDtypeStruct((b, m, d), BF16)] * 2,
        compiler_params=pltpu.CompilerParams(
            dimension_semantics=("parallel",), vmem_limit_bytes=VMEM_LIMIT_BYTES),
        name="memkv",
    )(mem, g.reshape(1, d), w_ckv.astype(BF16))


def _attn_kernel(q_ref, k_ref, v_ref, lq1_ref, lk1_ref, lq2_ref, lk2_ref, g_ref, o_ref,
                 qs_scr, sa_scr, sb_scr, m_scr, acc_scr):
    t = ATT_BLOCK
    rows = 2 * t
    qi = pl.program_id(2)
    q = q_ref[0]
    lane = lax.broadcasted_iota(jnp.int32, (t, LANES), 1)
    zero = jnp.zeros_like(q)
    qs_scr[0:t, :] = jnp.where(lane < DA_QKDIM, q, zero)
    qs_scr[t:rows, :] = jnp.where(lane >= DA_QKDIM, q, zero)
    ones_col = jnp.where(lane == 0, 1.0, 0.0).astype(BF16)

    m_scr[...] = jnp.full(m_scr.shape, NEG_BIG, F32)
    acc_scr[...] = jnp.zeros(acc_scr.shape, F32)

    def scores(j, dst, nsplit=1):
        k0 = pl.multiple_of(j * t, t)
        kc = k_ref[0, pl.ds(k0, t), :]
        rs = rows // nsplit
        for r in range(0, rows, rs):
            dst[r:r + rs, :] = lax.dot_general(qs_scr[r:r + rs, :], kc,
                                               (((1,), (1,)), ((), ())), preferred_element_type=F32)

    def consume(j, src, masked, nsplit=1):
        k0 = pl.multiple_of(j * t, t)
        vc = jnp.concatenate([v_ref[0, pl.ds(k0, t), :], ones_col], axis=1)
        s = src[...]
        if masked:
            row = lax.broadcasted_iota(jnp.int32, (rows, t), 0) % t
            col = lax.broadcasted_iota(jnp.int32, (rows, t), 1)
            s = jnp.where(col <= row, s, NEG_BIG)
        m_old = m_scr[...]
        m_new = jnp.maximum(m_old, jnp.max(s, axis=1, keepdims=True))
        alpha = jnp.exp2(m_old - m_new)
        p = jnp.exp2(s - jnp.tile(m_new, (1, t // LANES))).astype(BF16)
        rs = rows // nsplit
        for r in range(0, rows, rs):
            acc_scr[r:r + rs, :] = (jnp.tile(alpha[r:r + rs], (1, 2)) * acc_scr[r:r + rs, :]
                                    + jnp.dot(p[r:r + rs], vc, preferred_element_type=F32))
        m_scr[...] = m_new

    scores(0, sa_scr, nsplit=2)

    def run(j, nblk):
        for u in range(nblk):
            src, dst = (sa_scr, sb_scr) if u % 2 == 0 else (sb_scr, sa_scr)
            scores(j + u + 1, dst)
            consume(j + u, src, masked=False)

    def multi(i, carry):
        run(ATT_UNROLL * i, ATT_UNROLL)
        return carry

    lax.fori_loop(0, qi // ATT_UNROLL, multi, 0)
    done = (qi // ATT_UNROLL) * ATT_UNROLL

    def pair(i, carry):
        run(done + 2 * i, 2)
        return carry

    lax.fori_loop(0, (qi - done) // 2, pair, 0)

    @pl.when(qi % 2 == 1)
    def _():
        scores(qi, sb_scr)
        consume(qi - 1, sa_scr, masked=False)
        consume(qi, sb_scr, masked=True, nsplit=2)

    @pl.when(qi % 2 == 0)
    def _():
        consume(qi, sa_scr, masked=True, nsplit=2)

    lam = (jnp.exp(jnp.sum(lq1_ref[...] * lk1_ref[...], keepdims=True))
           - jnp.exp(jnp.sum(lq2_ref[...] * lk2_ref[...], keepdims=True)) + LAM_INIT)
    acc = acc_scr[...]
    o = acc[:, :DA_VDIM] / acc[:, DA_VDIM:DA_VDIM + 1]
    a = o[:t] - lam * o[t:]
    o_ref[0] = (_rms(a, g_ref[...]) * (1.0 - LAM_INIT)).astype(o_ref.dtype)


def _diff_attention(q, k, v, lq1, lk1, lq2, lk2, subln_g):
    b, s, _ = q.shape
    tq = ATT_BLOCK
    assert s % tq == 0
    vec = lambda n: pl.BlockSpec((1, n), lambda bi, h, i: (0, 0))
    return pl.pallas_call(
        _attn_kernel,
        grid=(b, N_DA_HEADS, s // tq),
        in_specs=[pl.BlockSpec((1, tq, LANES), lambda bi, h, i: (bi, i, h)),
                  pl.BlockSpec((1, s, LANES), lambda bi, h, i: (bi, 0, h)),
                  pl.BlockSpec((1, s, DA_VDIM), lambda bi, h, i: (bi, 0, h)),
                  vec(DA_QKDIM), vec(DA_QKDIM), vec(DA_QKDIM), vec(DA_QKDIM), vec(DA_VDIM)],
        out_specs=pl.BlockSpec((1, tq, DA_VDIM), lambda bi, h, i: (bi, i, h)),
        out_shape=jax.ShapeDtypeStruct((b, s, N_DA_HEADS * DA_VDIM), BF16),
        scratch_shapes=[pltpu.VMEM((2 * tq, LANES), BF16),
                        pltpu.VMEM((2 * tq, tq), F32), pltpu.VMEM((2 * tq, tq), F32),
                        pltpu.VMEM((2 * tq, LANES), F32),
                        pltpu.VMEM((2 * tq, 2 * DA_VDIM), F32)],
        compiler_params=pltpu.CompilerParams(
            dimension_semantics=("parallel", "parallel", "arbitrary"),
            vmem_limit_bytes=VMEM_LIMIT_BYTES),
        name="diffattn",
    )(q, k, v, lq1.reshape(1, -1), lk1.reshape(1, -1), lq2.reshape(1, -1), lk2.reshape(1, -1),
      subln_g.reshape(1, -1))


def _mixcross_kernel(x_ref, a_ref, c_ref, halo_ref, cw_ref, cb_ref, lg_ref, lb_ref, wo_ref,
                     ng_ref, wq_ref, mk_ref, mv_ref, wco_ref, o_ref, buf, shf, cact):
    t = x_ref.shape[1]
    d = x_ref.shape[2]
    i = pl.program_id(1)
    n = t + CV_HALO
    halo = halo_ref[0]
    buf[0:CV_HALO, :] = jnp.where(i > 0, halo, jnp.zeros_like(halo))
    buf[CV_HALO:, :] = c_ref[0]
    for p in range(1, SUBLANES):
        shf[p - 1, SUBLANES:n, :] = buf[SUBLANES - p:n - p, :]

    for r0 in range(0, t, CV_ROW_TILE):
        acc = jnp.zeros((CV_ROW_TILE, buf.shape[1]), F32) + cb_ref[...]
        for dl in range(CV_KERNEL):
            al, p = divmod(dl, SUBLANES)
            row = r0 + CV_HALO - al * SUBLANES
            src = buf if p == 0 else shf.at[p - 1]
            wj = CV_KERNEL - 1 - dl
            wt = jnp.tile(cw_ref[wj], (CV_ROW_TILE // SUBLANES, 1))
            acc = acc + wt * src[row:row + CV_ROW_TILE, :]
        mu = jnp.mean(acc, axis=-1, keepdims=True)
        xc = acc - mu
        var = jnp.mean(xc * xc, axis=-1, keepdims=True)
        y = xc * lax.rsqrt(var + EPS) * lg_ref[...] + lb_ref[...]
        cact[r0:r0 + CV_ROW_TILE, :] = (y * jax.nn.sigmoid(y)).astype(BF16)

    da = a_ref.shape[2]
    h1 = (x_ref[0]
          + jnp.dot(a_ref[0], wo_ref[0:da, :], preferred_element_type=F32)
          + jnp.dot(cact[...], wo_ref[da:, :], preferred_element_type=F32))

    hn = _rms(h1, ng_ref[...]).astype(BF16)
    qx = jnp.dot(hn, wq_ref[...], preferred_element_type=F32).astype(BF16)
    hd = d // N_X_HEADS
    heads = []
    for h in range(N_X_HEADS):
        sl = slice(h * hd, (h + 1) * hd)
        s = lax.dot_general(qx[:, sl], mk_ref[0, :, sl], (((1,), (1,)), ((), ())),
                            preferred_element_type=F32)
        s = s - jnp.max(s, axis=-1, keepdims=True)
        e = jnp.exp(s)
        p = e / jnp.sum(e, axis=-1, keepdims=True)
        heads.append(jnp.dot(p.astype(BF16), mv_ref[0, :, sl], preferred_element_type=F32))
    o = jnp.concatenate(heads, axis=-1).astype(BF16)
    o_ref[0] = h1 + jnp.dot(o, wco_ref[...], preferred_element_type=F32)


def _mixcross(x, a, c, cv_w, cv_b, ln_g, ln_b, w_out, ng, w_cq, mk, mv, w_co):
    b, s, d = x.shape
    t = min(TOK_BLOCK, s)
    da, cw, m = a.shape[2], c.shape[2], mk.shape[1]
    hpb = t // CV_HALO
    const = lambda shape: pl.BlockSpec(shape, lambda bi, i: (0,) * len(shape))
    tok = lambda w: pl.BlockSpec((1, t, w), lambda bi, i: (bi, i, 0))
    return pl.pallas_call(
        _mixcross_kernel,
        grid=(b, s // t),
        in_specs=[tok(d), tok(da), tok(cw),
                  pl.BlockSpec((1, CV_HALO, cw), lambda bi, i: (bi, jnp.maximum(i * hpb - 1, 0), 0)),
                  const((CV_KERNEL, SUBLANES, cw)), const((1, cw)), const((1, cw)), const((1, cw)),
                  const((da + cw, d)), const((1, d)), const((d, d)),
                  pl.BlockSpec((1, m, d), lambda bi, i: (bi, 0, 0)),
                  pl.BlockSpec((1, m, d), lambda bi, i: (bi, 0, 0)),
                  const((d, d))],
        out_specs=tok(d),
        out_shape=jax.ShapeDtypeStruct((b, s, d), F32),
        scratch_shapes=[pltpu.VMEM((t + CV_HALO, cw), F32),
                        pltpu.VMEM((SUBLANES - 1, t + CV_HALO, cw), F32),
                        pltpu.VMEM((t, cw), BF16)],
        compiler_params=pltpu.CompilerParams(
            dimension_semantics=("parallel", "parallel"), vmem_limit_bytes=VMEM_LIMIT_BYTES),
        name="mixcross",
    )(x, a, c, c, jnp.broadcast_to(cv_w[:, None, :], (CV_KERNEL, SUBLANES, cw)),
      cv_b.reshape(1, cw), ln_g.reshape(1, cw), ln_b.reshape(1, cw),
      w_out.astype(BF16), ng.reshape(1, d), w_cq.astype(BF16), mk, mv, w_co.astype(BF16))


def _convffn_kernel(h_ref, ng_ref, wg_ref, wv_ref, cg_ref, cv_ref, wd_ref, fg_ref, o_ref,
                    hn_scr, acc_scr, ga, va, gb, vb, gcarry, vcarry):
    t = h_ref.shape[1]
    i = pl.program_id(1)
    nchunk = wg_ref.shape[0]
    pad = SUBLANES

    @pl.when(i == 0)
    def _():
        gcarry[...] = jnp.zeros(gcarry.shape, F32)
        vcarry[...] = jnp.zeros(vcarry.shape, F32)

    hn_scr[...] = _rms(h_ref[0], ng_ref[...]).astype(BF16)
    acc_scr[...] = jnp.zeros(acc_scr.shape, F32)

    def up(f, gdst, vdst):
        hn = hn_scr[...]
        gdst[pad:, :] = jnp.dot(hn, wg_ref[f], preferred_element_type=F32)
        vdst[pad:, :] = jnp.dot(hn, wv_ref[f], preferred_element_type=F32)

    def conv(sbuf, w_ref, f, carry):
        sbuf[0:pad, :] = carry[f]
        carry[f] = sbuf[t:t + pad, :]
        w = w_ref[f]
        out = w[FFN_KERNEL - 1:FFN_KERNEL, :] * sbuf[pad:, :]
        for j in range(FFN_KERNEL - 1):
            off = pad - (FFN_KERNEL - 1) + j
            out = out + w[j:j + 1, :] * sbuf[off:off + t, :]
        return out

    def down(f, gsrc, vsrc):
        g = conv(gsrc, cg_ref, f, gcarry)
        v = conv(vsrc, cv_ref, f, vcarry)
        z = (g * jax.nn.sigmoid(g) * v).astype(BF16)
        acc_scr[...] += jnp.dot(z, wd_ref[f], preferred_element_type=F32)

    up(0, ga, va)

    def pair(k, carry):
        f = 2 * k
        up(f + 1, gb, vb)
        down(f, ga, va)
        up(f + 2, ga, va)
        down(f + 1, gb, vb)
        return carry

    lax.fori_loop(0, nchunk // 2, pair, 0)
    down(nchunk - 1, ga, va)
    o_ref[0] = _rms(h_ref[0] + acc_scr[...], fg_ref[...])


def _convffn(h, ng, w_up, dw_w, w_down, fg):
    b, s, d = h.shape
    t = min(TOK_BLOCK, s)
    dff = w_down.shape[0]
    fc = FFN_CHUNK
    nchunk = dff // fc
    assert dff % fc == 0 and nchunk % 2 == 1
    wu = w_up.astype(BF16).reshape(d, 2, nchunk, fc).transpose(1, 2, 0, 3)
    dw = dw_w.reshape(FFN_KERNEL, 2, nchunk, fc).transpose(1, 2, 0, 3)
    wd = w_down.astype(BF16).reshape(nchunk, fc, d)
    const = lambda shape: pl.BlockSpec(shape, lambda bi, i: (0,) * len(shape))
    tok = pl.BlockSpec((1, t, d), lambda bi, i: (bi, i, 0))
    return pl.pallas_call(
        _convffn_kernel,
        grid=(b, s // t),
        in_specs=[tok, const((1, d)), const((nchunk, d, fc)), const((nchunk, d, fc)),
                  const((nchunk, FFN_KERNEL, fc)), const((nchunk, FFN_KERNEL, fc)),
                  const((nchunk, fc, d)), const((1, d))],
        out_specs=tok,
        out_shape=jax.ShapeDtypeStruct((b, s, d), F32),
        scratch_shapes=[pltpu.VMEM((t, d), BF16), pltpu.VMEM((t, d), F32),
                        pltpu.VMEM((t + SUBLANES, fc), F32), pltpu.VMEM((t + SUBLANES, fc), F32),
                        pltpu.VMEM((t + SUBLANES, fc), F32), pltpu.VMEM((t + SUBLANES, fc), F32),
                        pltpu.VMEM((nchunk, SUBLANES, fc), F32),
                        pltpu.VMEM((nchunk, SUBLANES, fc), F32)],
        compiler_params=pltpu.CompilerParams(
            dimension_semantics=("parallel", "arbitrary"), vmem_limit_bytes=VMEM_LIMIT_BYTES),
        name="convffn",
    )(h, ng.reshape(1, d), wu[0], wu[1], dw[0], dw[1], wd, fg.reshape(1, d))


def kernel(x, mem, positions, norm_mix_g, w_in, lam_q1, lam_k1, lam_q2, lam_k2, subln_g, cv_dw_w, cv_dw_b, cv_ln_g, cv_ln_b, w_out, norm_cross_g, norm_mem_g, w_cq, w_ckv, w_co, norm_ffn_g, w_up, ffn_dw_w, w_down, norm_final_g):
    assert w_in.shape[0] == 1, "single-layer operation"
    q, k, v, c = _inproj(x, positions, norm_mix_g[0], w_in[0])
    mk, mv = _memkv(mem, norm_mem_g[0], w_ckv[0])
    a = _diff_attention(q, k, v, lam_q1[0], lam_k1[0], lam_q2[0], lam_k2[0], subln_g[0])
    h2 = _mixcross(x, a, c, cv_dw_w[0], cv_dw_b[0], cv_ln_g[0], cv_ln_b[0], w_out[0],
                   norm_cross_g[0], w_cq[0], mk, mv, w_co[0])
    return _convffn(h2, norm_ffn_g[0], w_up[0], ffn_dw_w[0], w_down[0], norm_final_g)
```

```python
import functools
import math

import jax
import jax.numpy as jnp
from jax import lax
from jax.experimental import pallas as pl
from jax.experimental.pallas import tpu as pltpu

F32 = jnp.float32
BF16 = jnp.bfloat16

N_DA_HEADS = 4
DA_VDIM = 128
DA_QKDIM = 64
ROT_DIM = 16
ROPE_THETA = 500000.0
CV_KERNEL = 31
N_X_HEADS = 4
FFN_KERNEL = 3
EPS = 1e-6
LAM_INIT = 0.8 - 0.6 * math.exp(-0.3 * 0)

LANES = 128
SUBLANES = 8
VMEM_LIMIT_BYTES = 56 * 1024 * 1024

NEG_BIG = -1e30

TOK_BLOCK = 512
ATT_UNROLLS = (8, 4, 2)
ATT_BLOCK = 512
CV_HALO = 32
CV_ROW_TILE = 32
FFN_CHUNK = 256


def _rms(x, g):
    ms = jnp.mean(x * x, axis=-1, keepdims=True)
    return x * lax.rsqrt(ms + EPS) * g


def _inproj_kernel(x_ref, pos_ref, g_ref, w_ref, freq_ref, ma_ref, mb_ref,
                   q_ref, k_ref, v_ref, c_ref):
    da = N_DA_HEADS * DA_VDIM
    xn = _rms(x_ref[0], g_ref[...]).astype(BF16)
    ang = pos_ref[0].astype(F32) * freq_ref[...]
    cos = jnp.cos(ang)
    sin = jnp.sin(ang)
    sa = sin * ma_ref[...]
    sb = sin * mb_ref[...]

    def rope(t):
        return (t * cos + pltpu.roll(t, ROT_DIM // 2, 1) * sa
                + pltpu.roll(t, LANES - ROT_DIM // 2, 1) * sb)

    qp = jnp.dot(xn, w_ref[:, 0:da], preferred_element_type=F32)
    kp = jnp.dot(xn, w_ref[:, da:2 * da], preferred_element_type=F32)
    scale = math.log2(math.e) / math.sqrt(DA_QKDIM)
    for h in range(N_DA_HEADS):
        sl = slice(h * LANES, (h + 1) * LANES)
        q_ref[0, :, sl] = (rope(qp[:, sl]) * scale).astype(BF16)
        k_ref[0, :, sl] = rope(kp[:, sl]).astype(BF16)
    v_ref[0] = jnp.dot(xn, w_ref[:, 2 * da:3 * da], preferred_element_type=F32).astype(BF16)
    cw = (w_ref.shape[1] - 3 * da) // 2
    uv = jnp.dot(xn, w_ref[:, 3 * da:3 * da + cw], preferred_element_type=F32)
    ug = jnp.dot(xn, w_ref[:, 3 * da + cw:], preferred_element_type=F32)
    c_ref[0] = uv * jax.nn.sigmoid(ug)


def _inproj(x, positions, g, w_in):
    b, s, d = x.shape
    t = min(TOK_BLOCK, s)
    da = N_DA_HEADS * DA_VDIM
    cw = (w_in.shape[1] - 3 * da) // 2
    inv_freq = ROPE_THETA ** (-jnp.arange(0, ROT_DIM, 2, dtype=F32) / ROT_DIM)
    j = jnp.arange(LANES) % DA_QKDIM
    half = ROT_DIM // 2
    freq = jnp.where(j < ROT_DIM, inv_freq[j % half], 0.0).astype(F32)[None, :]
    ma = jnp.where((j >= half) & (j < ROT_DIM), 1.0, 0.0).astype(F32)[None, :]
    mb = jnp.where(j < half, -1.0, 0.0).astype(F32)[None, :]
    const = lambda shape: pl.BlockSpec(shape, lambda bi, i: (0,) * len(shape))
    tok = lambda w: pl.BlockSpec((1, t, w), lambda bi, i: (bi, i, 0))
    return pl.pallas_call(
        _inproj_kernel,
        grid=(b, s // t),
        in_specs=[tok(d), tok(1), const((1, d)), const(w_in.shape),
                  const((1, LANES)), const((1, LANES)), const((1, LANES))],
        out_specs=[tok(da), tok(da), tok(da), tok(cw)],
        out_shape=[jax.ShapeDtypeStruct((b, s, da), BF16)] * 3
                  + [jax.ShapeDtypeStruct((b, s, cw), F32)],
        compiler_params=pltpu.CompilerParams(
            dimension_semantics=("parallel", "parallel"), vmem_limit_bytes=VMEM_LIMIT_BYTES),
        name="inproj",
    )(x, positions.reshape(b, s, 1), g.reshape(1, d), w_in.astype(BF16), freq, ma, mb)


def _memkv_kernel(m_ref, g_ref, w_ref, k_ref, v_ref):
    d = m_ref.shape[2]
    mn = _rms(m_ref[0], g_ref[...]).astype(BF16)
    kv = jnp.dot(mn, w_ref[...], preferred_element_type=F32)
    scale = 1.0 / math.sqrt(d // N_X_HEADS)
    k_ref[0] = (kv[:, :d] * scale).astype(BF16)
    v_ref[0] = kv[:, d:].astype(BF16)


def _memkv(mem, g, w_ckv):
    b, m, d = mem.shape
    return pl.pallas_call(
        _memkv_kernel,
        grid=(b,),
        in_specs=[pl.BlockSpec((1, m, d), lambda bi: (bi, 0, 0)),
                  pl.BlockSpec((1, d), lambda bi: (0, 0)),
                  pl.BlockSpec((d, 2 * d), lambda bi: (0, 0))],
        out_specs=[pl.BlockSpec((1, m, d), lambda bi: (bi, 0, 0))] * 2,
        out_shape=[jax.ShapeDtypeStruct((b, m, d), BF16)] * 2,
        compiler_params=pltpu.CompilerParams(
            dimension_semantics=("parallel",), vmem_limit_bytes=VMEM_LIMIT_BYTES),
        name="memkv",
    )(mem, g.reshape(1, d), w_ckv.astype(BF16))


def _attn_kernel(q_ref, qn_ref, k_ref, v_ref, lq1_ref, lk1_ref, lq2_ref, lk2_ref, g_ref, o_ref,
                 qs_scr, sa_scr, sb_scr, m_scr, acc_scr):
    t = ATT_BLOCK
    rows = 2 * t
    qi = pl.program_id(2)
    lane = lax.broadcasted_iota(jnp.int32, (t, LANES), 1)
    ones_col = jnp.where(lane == 0, 1.0, 0.0).astype(BF16)

    def stack_q(q):
        zero = jnp.zeros_like(q)
        qs_scr[0:t, :] = jnp.where(lane < DA_QKDIM, q, zero)
        qs_scr[t:rows, :] = jnp.where(lane >= DA_QKDIM, q, zero)

    m_scr[...] = jnp.full(m_scr.shape, NEG_BIG, F32)
    acc_scr[...] = jnp.zeros(acc_scr.shape, F32)

    def scores(j, dst, nsplit=1):
        k0 = pl.multiple_of(j * t, t)
        kc = k_ref[0, pl.ds(k0, t), :]
        rs = rows // nsplit
        for r in range(0, rows, rs):
            dst[r:r + rs, :] = lax.dot_general(qs_scr[r:r + rs, :], kc,
                                               (((1,), (1,)), ((), ())), preferred_element_type=F32)

    def consume(j, src, masked, nsplit=1):
        k0 = pl.multiple_of(j * t, t)
        vc = jnp.concatenate([v_ref[0, pl.ds(k0, t), :], ones_col], axis=1)
        s = src[...]
        if masked:
            row = lax.broadcasted_iota(jnp.int32, (rows, t), 0) % t
            col = lax.broadcasted_iota(jnp.int32, (rows, t), 1)
            s = jnp.where(col <= row, s, NEG_BIG)
        m_old = m_scr[...]
        m_new = jnp.maximum(m_old, jnp.max(s, axis=1, keepdims=True))
        alpha = jnp.exp2(m_old - m_new)
        p = jnp.exp2(s - jnp.tile(m_new, (1, t // LANES))).astype(BF16)
        rs = rows // nsplit
        for r in range(0, rows, rs):
            acc_scr[r:r + rs, :] = (jnp.tile(alpha[r:r + rs], (1, 2)) * acc_scr[r:r + rs, :]
                                    + jnp.dot(p[r:r + rs], vc, preferred_element_type=F32))
        m_scr[...] = m_new

    @pl.when(qi == 0)
    def _():
        stack_q(q_ref[0])
        scores(0, sa_scr, nsplit=2)

    def run(j, nblk):
        for u in range(nblk):
            src, dst = (sa_scr, sb_scr) if u % 2 == 0 else (sb_scr, sa_scr)
            scores(j + u + 1, dst)
            consume(j + u, src, masked=False)

    done = 0
    for unroll in ATT_UNROLLS:
        def body(i, carry, unroll=unroll, done=done):
            run(done + unroll * i, unroll)
            return carry

        trips = (qi - done) // unroll
        lax.fori_loop(0, trips, body, 0)
        done = done + trips * unroll

    @pl.when(qi % 2 == 1)
    def _():
        scores(qi, sb_scr)
        consume(qi - 1, sa_scr, masked=False)
        consume(qi, sb_scr, masked=True, nsplit=2)

    @pl.when(qi % 2 == 0)
    def _():
        consume(qi, sa_scr, masked=True, nsplit=2)

    lam = (jnp.exp(jnp.sum(lq1_ref[...] * lk1_ref[...], keepdims=True))
           - jnp.exp(jnp.sum(lq2_ref[...] * lk2_ref[...], keepdims=True)) + LAM_INIT)
    acc = acc_scr[...]
    o = acc[:, :DA_VDIM] / acc[:, DA_VDIM:DA_VDIM + 1]
    a = o[:t] - lam * o[t:]
    o_ref[0] = (_rms(a, g_ref[...]) * (1.0 - LAM_INIT)).astype(o_ref.dtype)

    stack_q(qn_ref[0])
    scores(0, sa_scr, nsplit=2)


def _diff_attention(q, k, v, lq1, lk1, lq2, lk2, subln_g):
    b, s, _ = q.shape
    tq = ATT_BLOCK
    assert s % tq == 0
    nq = s // tq
    vec = lambda n: pl.BlockSpec((1, n), lambda bi, h, i: (0, 0))
    return pl.pallas_call(
        _attn_kernel,
        grid=(b, N_DA_HEADS, s // tq),
        in_specs=[pl.BlockSpec((1, tq, LANES), lambda bi, h, i: (bi, i, h)),
                  pl.BlockSpec((1, tq, LANES), lambda bi, h, i: (bi, jnp.minimum(i + 1, nq - 1), h)),
                  pl.BlockSpec((1, s, LANES), lambda bi, h, i: (bi, 0, h)),
                  pl.BlockSpec((1, s, DA_VDIM), lambda bi, h, i: (bi, 0, h)),
                  vec(DA_QKDIM), vec(DA_QKDIM), vec(DA_QKDIM), vec(DA_QKDIM), vec(DA_VDIM)],
        out_specs=pl.BlockSpec((1, tq, DA_VDIM), lambda bi, h, i: (bi, i, h)),
        out_shape=jax.ShapeDtypeStruct((b, s, N_DA_HEADS * DA_VDIM), BF16),
        scratch_shapes=[pltpu.VMEM((2 * tq, LANES), BF16),
                        pltpu.VMEM((2 * tq, tq), F32), pltpu.VMEM((2 * tq, tq), F32),
                        pltpu.VMEM((2 * tq, LANES), F32),
                        pltpu.VMEM((2 * tq, 2 * DA_VDIM), F32)],
        compiler_params=pltpu.CompilerParams(
            dimension_semantics=("parallel", "parallel", "arbitrary"),
            vmem_limit_bytes=VMEM_LIMIT_BYTES),
        name="diffattn",
    )(q, q, k, v, lq1.reshape(1, -1), lk1.reshape(1, -1), lq2.reshape(1, -1), lk2.reshape(1, -1),
      subln_g.reshape(1, -1))


def _mixcross_kernel(x_ref, a_ref, c_ref, halo_ref, cw_ref, cb_ref, lg_ref, lb_ref, wo_ref,
                     ng_ref, wq_ref, mk_ref, mv_ref, wco_ref, o_ref, buf, shf, cact):
    t = x_ref.shape[1]
    d = x_ref.shape[2]
    i = pl.program_id(1)
    n = t + CV_HALO
    halo = halo_ref[0]
    buf[0:CV_HALO, :] = jnp.where(i > 0, halo, jnp.zeros_like(halo))
    buf[CV_HALO:, :] = c_ref[0]
    for p in range(1, SUBLANES):
        shf[p - 1, SUBLANES:n, :] = buf[SUBLANES - p:n - p, :]

    for r0 in range(0, t, CV_ROW_TILE):
        acc = jnp.zeros((CV_ROW_TILE, buf.shape[1]), F32) + cb_ref[...]
        for dl in range(CV_KERNEL):
            al, p = divmod(dl, SUBLANES)
            row = r0 + CV_HALO - al * SUBLANES
            src = buf if p == 0 else shf.at[p - 1]
            wj = CV_KERNEL - 1 - dl
            wt = jnp.tile(cw_ref[wj], (CV_ROW_TILE // SUBLANES, 1))
            acc = acc + wt * src[row:row + CV_ROW_TILE, :]
        mu = jnp.mean(acc, axis=-1, keepdims=True)
        xc = acc - mu
        var = jnp.mean(xc * xc, axis=-1, keepdims=True)
        y = xc * lax.rsqrt(var + EPS) * lg_ref[...] + lb_ref[...]
        cact[r0:r0 + CV_ROW_TILE, :] = (y * jax.nn.sigmoid(y)).astype(BF16)

    da = a_ref.shape[2]
    h1 = (x_ref[0]
          + jnp.dot(a_ref[0], wo_ref[0:da, :], preferred_element_type=F32)
          + jnp.dot(cact[...], wo_ref[da:, :], preferred_element_type=F32))

    hn = _rms(h1, ng_ref[...]).astype(BF16)
    qx = jnp.dot(hn, wq_ref[...], preferred_element_type=F32).astype(BF16)
    hd = d // N_X_HEADS
    heads = []
    for h in range(N_X_HEADS):
        sl = slice(h * hd, (h + 1) * hd)
        s = lax.dot_general(qx[:, sl], mk_ref[0, :, sl], (((1,), (1,)), ((), ())),
                            preferred_element_type=F32)
        s = s - jnp.max(s, axis=-1, keepdims=True)
        e = jnp.exp(s)
        p = e / jnp.sum(e, axis=-1, keepdims=True)
        heads.append(jnp.dot(p.astype(BF16), mv_ref[0, :, sl], preferred_element_type=F32))
    o = jnp.concatenate(heads, axis=-1).astype(BF16)
    o_ref[0] = h1 + jnp.dot(o, wco_ref[...], preferred_element_type=F32)


def _mixcross(x, a, c, cv_w, cv_b, ln_g, ln_b, w_out, ng, w_cq, mk, mv, w_co):
    b, s, d = x.shape
    t = min(TOK_BLOCK, s)
    da, cw, m = a.shape[2], c.shape[2], mk.shape[1]
    hpb = t // CV_HALO
    const = lambda shape: pl.BlockSpec(shape, lambda bi, i: (0,) * len(shape))
    tok = lambda w: pl.BlockSpec((1, t, w), lambda bi, i: (bi, i, 0))
    return pl.pallas_call(
        _mixcross_kernel,
        grid=(b, s // t),
        in_specs=[tok(d), tok(da), tok(cw),
                  pl.BlockSpec((1, CV_HALO, cw), lambda bi, i: (bi, jnp.maximum(i * hpb - 1, 0), 0)),
                  const((CV_KERNEL, SUBLANES, cw)), const((1, cw)), const((1, cw)), const((1, cw)),
                  const((da + cw, d)), const((1, d)), const((d, d)),
                  pl.BlockSpec((1, m, d), lambda bi, i: (bi, 0, 0)),
                  pl.BlockSpec((1, m, d), lambda bi, i: (bi, 0, 0)),
                  const((d, d))],
        out_specs=tok(d),
        out_shape=jax.ShapeDtypeStruct((b, s, d), F32),
        scratch_shapes=[pltpu.VMEM((t + CV_HALO, cw), F32),
                        pltpu.VMEM((SUBLANES - 1, t + CV_HALO, cw), F32),
                        pltpu.VMEM((t, cw), BF16)],
        compiler_params=pltpu.CompilerParams(
            dimension_semantics=("parallel", "parallel"), vmem_limit_bytes=VMEM_LIMIT_BYTES),
        name="mixcross",
    )(x, a, c, c, jnp.broadcast_to(cv_w[:, None, :], (CV_KERNEL, SUBLANES, cw)),
      cv_b.reshape(1, cw), ln_g.reshape(1, cw), ln_b.reshape(1, cw),
      w_out.astype(BF16), ng.reshape(1, d), w_cq.astype(BF16), mk, mv, w_co.astype(BF16))


def _convffn_kernel(h_ref, ng_ref, wg_ref, wv_ref, cg_ref, cv_ref, wd_ref, fg_ref, o_ref,
                    hn_scr, acc_scr, ga, va, gb, vb, gcarry, vcarry):
    t = h_ref.shape[1]
    i = pl.program_id(1)
    nchunk = wg_ref.shape[0]
    pad = SUBLANES

    @pl.when(i == 0)
    def _():
        gcarry[...] = jnp.zeros(gcarry.shape, F32)
        vcarry[...] = jnp.zeros(vcarry.shape, F32)

    hn_scr[...] = _rms(h_ref[0], ng_ref[...]).astype(BF16)
    acc_scr[...] = jnp.zeros(acc_scr.shape, F32)

    def up(f, gdst, vdst):
        hn = hn_scr[...]
        gdst[pad:, :] = jnp.dot(hn, wg_ref[f], preferred_element_type=F32)
        vdst[pad:, :] = jnp.dot(hn, wv_ref[f], preferred_element_type=F32)

    def conv(sbuf, w_ref, f, carry):
        sbuf[0:pad, :] = carry[f]
        carry[f] = sbuf[t:t + pad, :]
        w = w_ref[f]
        out = w[FFN_KERNEL - 1:FFN_KERNEL, :] * sbuf[pad:, :]
        for j in range(FFN_KERNEL - 1):
            off = pad - (FFN_KERNEL - 1) + j
            out = out + w[j:j + 1, :] * sbuf[off:off + t, :]
        return out

    def down(f, gsrc, vsrc):
        g = conv(gsrc, cg_ref, f, gcarry)
        v = conv(vsrc, cv_ref, f, vcarry)
        z = (g * jax.nn.sigmoid(g) * v).astype(BF16)
        acc_scr[...] += jnp.dot(z, wd_ref[f], preferred_element_type=F32)

    up(0, ga, va)

    def pair(k, carry):
        f = 2 * k
        up(f + 1, gb, vb)
        down(f, ga, va)
        up(f + 2, ga, va)
        down(f + 1, gb, vb)
        return carry

    lax.fori_loop(0, nchunk // 2, pair, 0)
    down(nchunk - 1, ga, va)
    o_ref[0] = _rms(h_ref[0] + acc_scr[...], fg_ref[...])


def _convffn(h, ng, w_up, dw_w, w_down, fg):
    b, s, d = h.shape
    t = min(TOK_BLOCK, s)
    dff = w_down.shape[0]
    fc = FFN_CHUNK
    nchunk = dff // fc
    assert dff % fc == 0 and nchunk % 2 == 1
    wu = w_up.astype(BF16).reshape(d, 2, nchunk, fc).transpose(1, 2, 0, 3)
    dw = dw_w.reshape(FFN_KERNEL, 2, nchunk, fc).transpose(1, 2, 0, 3)
    wd = w_down.astype(BF16).reshape(nchunk, fc, d)
    const = lambda shape: pl.BlockSpec(shape, lambda bi, i: (0,) * len(shape))
    tok = pl.BlockSpec((1, t, d), lambda bi, i: (bi, i, 0))
    return pl.pallas_call(
        _convffn_kernel,
        grid=(b, s // t),
        in_specs=[tok, const((1, d)), const((nchunk, d, fc)), const((nchunk, d, fc)),
                  const((nchunk, FFN_KERNEL, fc)), const((nchunk, FFN_KERNEL, fc)),
                  const((nchunk, fc, d)), const((1, d))],
        out_specs=tok,
        out_shape=jax.ShapeDtypeStruct((b, s, d), F32),
        scratch_shapes=[pltpu.VMEM((t, d), BF16), pltpu.VMEM((t, d), F32),
                        pltpu.VMEM((t + SUBLANES, fc), F32), pltpu.VMEM((t + SUBLANES, fc), F32),
                        pltpu.VMEM((t + SUBLANES, fc), F32), pltpu.VMEM((t + SUBLANES, fc), F32),
                        pltpu.VMEM((nchunk, SUBLANES, fc), F32),
                        pltpu.VMEM((nchunk, SUBLANES, fc), F32)],
        compiler_params=pltpu.CompilerParams(
            dimension_semantics=("parallel", "arbitrary"), vmem_limit_bytes=VMEM_LIMIT_BYTES),
        name="convffn",
    )(h, ng.reshape(1, d), wu[0], wu[1], dw[0], dw[1], wd, fg.reshape(1, d))


def kernel(x, mem, positions, norm_mix_g, w_in, lam_q1, lam_k1, lam_q2, lam_k2, subln_g, cv_dw_w, cv_dw_b, cv_ln_g, cv_ln_b, w_out, norm_cross_g, norm_mem_g, w_cq, w_ckv, w_co, norm_ffn_g, w_up, ffn_dw_w, w_down, norm_final_g):
    assert w_in.shape[0] == 1, "single-layer operation"
    q, k, v, c = _inproj(x, positions, norm_mix_g[0], w_in[0])
    mk, mv = _memkv(mem, norm_mem_g[0], w_ckv[0])
    a = _diff_attention(q, k, v, lam_q1[0], lam_k1[0], lam_q2[0], lam_k2[0], subln_g[0])
    h2 = _mixcross(x, a, c, cv_dw_w[0], cv_dw_b[0], cv_ln_g[0], cv_ln_b[0], w_out[0],
                   norm_cross_g[0], w_cq[0], mk, mv, w_co[0])
    return _convffn(h2, norm_ffn_g[0], w_up[0], ffn_dw_w[0], w_down[0], norm_final_g)
```

```python
import functools
import math

import jax
import jax.numpy as jnp
from jax import lax
from jax.experimental import pallas as pl
from jax.experimental.pallas import tpu as pltpu

F32 = jnp.float32
BF16 = jnp.bfloat16

N_DA_HEADS = 4
DA_VDIM = 128
DA_QKDIM = 64
ROT_DIM = 16
ROPE_THETA = 500000.0
CV_KERNEL = 31
N_X_HEADS = 4
FFN_KERNEL = 3
EPS = 1e-6
LAM_INIT = 0.8 - 0.6 * math.exp(-0.3 * 0)

LANES = 128
SUBLANES = 8
VMEM_LIMIT_BYTES = 56 * 1024 * 1024

NEG_BIG = -1e30

TOK_BLOCK = 512
ATT_UNROLLS = (8, 4, 2)
ATT_BLOCK = 512
CV_HALO = 32
CV_ROW_TILE = 32
FFN_CHUNK = 256


def _rms(x, g):
    ms = jnp.mean(x * x, axis=-1, keepdims=True)
    return x * lax.rsqrt(ms + EPS) * g


def _inproj_kernel(x_ref, pos_ref, g_ref, w_ref, freq_ref, sel_ref, one_ref,
                   q_ref, k_ref, v_ref, c_ref):
    da = N_DA_HEADS * DA_VDIM
    xn = _rms(x_ref[0], g_ref[...]).astype(BF16)
    ang = pos_ref[0].astype(F32) * freq_ref[...]

    def split3(v):
        v1 = v.astype(BF16)
        r1 = v - v1.astype(F32)
        v2 = r1.astype(BF16)
        v3 = (r1 - v2.astype(F32)).astype(BF16)
        return [v1, v2, v3]

    trig = jnp.concatenate(split3(jnp.cos(ang)) + split3(jnp.sin(ang)), axis=0)
    tab = lax.dot_general(trig, sel_ref[...], (((0,), (0,)), ((), ())),
                          preferred_element_type=F32)
    cos = tab[:, 0:LANES] + one_ref[...]
    sa = tab[:, LANES:2 * LANES]
    sb = tab[:, 2 * LANES:3 * LANES]

    def rope(t):
        return (t * cos + pltpu.roll(t, ROT_DIM // 2, 1) * sa
                + pltpu.roll(t, LANES - ROT_DIM // 2, 1) * sb)

    cw = (w_ref.shape[1] - 3 * da) // 2
    uv = jnp.dot(xn, w_ref[:, 3 * da:3 * da + cw], preferred_element_type=F32)
    ug = jnp.dot(xn, w_ref[:, 3 * da + cw:], preferred_element_type=F32)
    c_ref[0] = uv * jax.nn.sigmoid(ug)
    qp = jnp.dot(xn, w_ref[:, 0:da], preferred_element_type=F32)
    kp = jnp.dot(xn, w_ref[:, da:2 * da], preferred_element_type=F32)
    scale = math.log2(math.e) / math.sqrt(DA_QKDIM)
    for h in range(N_DA_HEADS):
        sl = slice(h * LANES, (h + 1) * LANES)
        q_ref[0, :, sl] = (rope(qp[:, sl]) * scale).astype(BF16)
        k_ref[0, :, sl] = rope(kp[:, sl]).astype(BF16)
    v_ref[0] = jnp.dot(xn, w_ref[:, 2 * da:3 * da], preferred_element_type=F32).astype(BF16)


def _inproj(x, positions, g, w_in):
    b, s, d = x.shape
    t = min(TOK_BLOCK, s)
    da = N_DA_HEADS * DA_VDIM
    cw = (w_in.shape[1] - 3 * da) // 2
    half = ROT_DIM // 2
    inv_freq = ROPE_THETA ** (-jnp.arange(0, ROT_DIM, 2, dtype=F32) / ROT_DIM)
    freq = jnp.broadcast_to(inv_freq[:, None], (half, t))
    j = jnp.arange(LANES) % DA_QKDIM
    hit = (jnp.arange(half)[:, None] == (j % half)[None, :]) & (j < ROT_DIM)[None, :]
    e_cos = hit.astype(F32)
    e_sa = (hit & (j >= half)[None, :]).astype(F32)
    e_sb = -(hit & (j < half)[None, :]).astype(F32)
    zero = jnp.zeros_like(e_cos)
    sel = jnp.concatenate([jnp.concatenate([e_cos, zero, zero], axis=1)] * 3
                          + [jnp.concatenate([zero, e_sa, e_sb], axis=1)] * 3, axis=0).astype(BF16)
    one = jnp.where(j < ROT_DIM, 0.0, 1.0).astype(F32)[None, :]
    const = lambda shape: pl.BlockSpec(shape, lambda bi, i: (0,) * len(shape))
    tok = lambda w: pl.BlockSpec((1, t, w), lambda bi, i: (bi, i, 0))
    return pl.pallas_call(
        _inproj_kernel,
        grid=(b, s // t),
        in_specs=[tok(d), pl.BlockSpec((1, 1, t), lambda bi, i: (bi * (s // t) + i, 0, 0)),
                  const((1, d)), const(w_in.shape),
                  const((half, t)), const((6 * half, 3 * LANES)), const((1, LANES))],
        out_specs=[tok(da), tok(da), tok(da), tok(cw)],
        out_shape=[jax.ShapeDtypeStruct((b, s, da), BF16)] * 3
                  + [jax.ShapeDtypeStruct((b, s, cw), F32)],
        compiler_params=pltpu.CompilerParams(
            dimension_semantics=("parallel", "parallel"), vmem_limit_bytes=VMEM_LIMIT_BYTES),
        name="inproj",
    )(x, positions.reshape(b * (s // t), 1, t), g.reshape(1, d), w_in.astype(BF16), freq, sel, one)


def _memkv_kernel(m_ref, g_ref, w_ref, k_ref, v_ref):
    d = m_ref.shape[2]
    mn = _rms(m_ref[0], g_ref[...]).astype(BF16)
    kv = jnp.dot(mn, w_ref[...], preferred_element_type=F32)
    scale = math.log2(math.e) / math.sqrt(d // N_X_HEADS)
    k_ref[0] = (kv[:, :d] * scale).astype(BF16)
    v_ref[0] = kv[:, d:].astype(BF16)


def _memkv(mem, g, w_ckv):
    b, m, d = mem.shape
    return pl.pallas_call(
        _memkv_kernel,
        grid=(b,),
        in_specs=[pl.BlockSpec((1, m, d), lambda bi: (bi, 0, 0)),
                  pl.BlockSpec((1, d), lambda bi: (0, 0)),
                  pl.BlockSpec((d, 2 * d), lambda bi: (0, 0))],
        out_specs=[pl.BlockSpec((1, m, d), lambda bi: (bi, 0, 0))] * 2,
        out_shape=[jax.ShapeDtypeStruct((b, m, d), BF16)] * 2,
        compiler_params=pltpu.CompilerParams(
            dimension_semantics=("parallel",), vmem_limit_bytes=VMEM_LIMIT_BYTES),
        name="memkv",
    )(mem, g.reshape(1, d), w_ckv.astype(BF16))


def _attn_kernel(q_ref, qn_ref, k_ref, v_ref, lq1_ref, lk1_ref, lq2_ref, lk2_ref, g_ref, o_ref,
                 qs_scr, sa_scr, sb_scr, m_scr, acc_scr):
    t = ATT_BLOCK
    rows = 2 * t
    qi = pl.program_id(2)
    lane = lax.broadcasted_iota(jnp.int32, (t, LANES), 1)
    ones_col = jnp.where(lane == 0, 1.0, 0.0).astype(BF16)

    def stack_q(q):
        zero = jnp.zeros_like(q)
        qs_scr[0:t, :] = jnp.where(lane < DA_QKDIM, q, zero)
        qs_scr[t:rows, :] = jnp.where(lane >= DA_QKDIM, q, zero)

    m_scr[...] = jnp.full(m_scr.shape, NEG_BIG, F32)
    acc_scr[...] = jnp.zeros(acc_scr.shape, F32)

    def scores(j, dst, nsplit=1):
        k0 = pl.multiple_of(j * t, t)
        kc = k_ref[0, pl.ds(k0, t), :]
        rs = rows // nsplit
        for r in range(0, rows, rs):
            dst[r:r + rs, :] = lax.dot_general(qs_scr[r:r + rs, :], kc,
                                               (((1,), (1,)), ((), ())), preferred_element_type=F32)

    def consume(j, src, masked, nsplit=1):
        k0 = pl.multiple_of(j * t, t)
        vc = jnp.concatenate([v_ref[0, pl.ds(k0, t), :], ones_col], axis=1)
        s = src[...]
        if masked:
            row = lax.broadcasted_iota(jnp.int32, (rows, t), 0) % t
            col = lax.broadcasted_iota(jnp.int32, (rows, t), 1)
            s = jnp.where(col <= row, s, NEG_BIG)
        m_old = m_scr[...]
        m_new = jnp.maximum(m_old, jnp.max(s, axis=1, keepdims=True))
        alpha = jnp.exp2(m_old - m_new)
        p = jnp.exp2(s - jnp.tile(m_new, (1, t // LANES))).astype(BF16)
        rs = rows // nsplit
        for r in range(0, rows, rs):
            acc_scr[r:r + rs, :] = (jnp.tile(alpha[r:r + rs], (1, 2)) * acc_scr[r:r + rs, :]
                                    + jnp.dot(p[r:r + rs], vc, preferred_element_type=F32))
        m_scr[...] = m_new

    @pl.when(qi == 0)
    def _():
        stack_q(q_ref[0])
        scores(0, sa_scr, nsplit=2)

    def run(j, nblk):
        for u in range(nblk):
            src, dst = (sa_scr, sb_scr) if u % 2 == 0 else (sb_scr, sa_scr)
            scores(j + u + 1, dst)
            consume(j + u, src, masked=False)

    done = 0
    for unroll in ATT_UNROLLS:
        def body(i, carry, unroll=unroll, done=done):
            run(done + unroll * i, unroll)
            return carry

        trips = (qi - done) // unroll
        lax.fori_loop(0, trips, body, 0)
        done = done + trips * unroll

    @pl.when(qi % 2 == 1)
    def _():
        scores(qi, sb_scr)
        consume(qi - 1, sa_scr, masked=False)
        consume(qi, sb_scr, masked=True, nsplit=2)

    @pl.when(qi % 2 == 0)
    def _():
        consume(qi, sa_scr, masked=True, nsplit=2)

    lam = (jnp.exp(jnp.sum(lq1_ref[...] * lk1_ref[...], keepdims=True))
           - jnp.exp(jnp.sum(lq2_ref[...] * lk2_ref[...], keepdims=True)) + LAM_INIT)
    acc = acc_scr[...]
    o = acc[:, :DA_VDIM] / acc[:, DA_VDIM:DA_VDIM + 1]
    a = o[:t] - lam * o[t:]
    o_ref[0] = (_rms(a, g_ref[...]) * (1.0 - LAM_INIT)).astype(o_ref.dtype)

    stack_q(qn_ref[0])
    scores(0, sa_scr, nsplit=2)


def _diff_attention(q, k, v, lq1, lk1, lq2, lk2, subln_g):
    b, s, _ = q.shape
    tq = ATT_BLOCK
    assert s % tq == 0
    nq = s // tq
    vec = lambda n: pl.BlockSpec((1, n), lambda bi, h, i: (0, 0))
    return pl.pallas_call(
        _attn_kernel,
        grid=(b, N_DA_HEADS, s // tq),
        in_specs=[pl.BlockSpec((1, tq, LANES), lambda bi, h, i: (bi, i, h)),
                  pl.BlockSpec((1, tq, LANES), lambda bi, h, i: (bi, jnp.minimum(i + 1, nq - 1), h)),
                  pl.BlockSpec((1, s, LANES), lambda bi, h, i: (bi, 0, h)),
                  pl.BlockSpec((1, s, DA_VDIM), lambda bi, h, i: (bi, 0, h)),
                  vec(DA_QKDIM), vec(DA_QKDIM), vec(DA_QKDIM), vec(DA_QKDIM), vec(DA_VDIM)],
        out_specs=pl.BlockSpec((1, tq, DA_VDIM), lambda bi, h, i: (bi, i, h)),
        out_shape=jax.ShapeDtypeStruct((b, s, N_DA_HEADS * DA_VDIM), BF16),
        scratch_shapes=[pltpu.VMEM((2 * tq, LANES), BF16),
                        pltpu.VMEM((2 * tq, tq), F32), pltpu.VMEM((2 * tq, tq), F32),
                        pltpu.VMEM((2 * tq, LANES), F32),
                        pltpu.VMEM((2 * tq, 2 * DA_VDIM), F32)],
        compiler_params=pltpu.CompilerParams(
            dimension_semantics=("parallel", "parallel", "arbitrary"),
            vmem_limit_bytes=VMEM_LIMIT_BYTES),
        name="diffattn",
    )(q, q, k, v, lq1.reshape(1, -1), lk1.reshape(1, -1), lq2.reshape(1, -1), lk2.reshape(1, -1),
      subln_g.reshape(1, -1))


def _mixcross_kernel(x_ref, a_ref, c_ref, halo_ref, cw_ref, cb_ref, lg_ref, lb_ref, wo_ref,
                     ng_ref, wq_ref, mk_ref, mv_ref, wco_ref, o_ref, buf, shf, cact):
    t = x_ref.shape[1]
    d = x_ref.shape[2]
    i = pl.program_id(1)
    n = t + CV_HALO
    halo = halo_ref[0]
    buf[0:CV_HALO, :] = jnp.where(i > 0, halo, jnp.zeros_like(halo))
    buf[CV_HALO:, :] = c_ref[0]
    for p in range(1, SUBLANES):
        shf[p - 1, SUBLANES:n, :] = buf[SUBLANES - p:n - p, :]

    for r0 in range(0, t, CV_ROW_TILE):
        acc = jnp.zeros((CV_ROW_TILE, buf.shape[1]), F32) + cb_ref[...]
        for dl in range(CV_KERNEL):
            al, p = divmod(dl, SUBLANES)
            row = r0 + CV_HALO - al * SUBLANES
            src = buf if p == 0 else shf.at[p - 1]
            wj = CV_KERNEL - 1 - dl
            wt = jnp.tile(cw_ref[wj], (CV_ROW_TILE // SUBLANES, 1))
            acc = acc + wt * src[row:row + CV_ROW_TILE, :]
        mu = jnp.mean(acc, axis=-1, keepdims=True)
        xc = acc - mu
        var = jnp.mean(xc * xc, axis=-1, keepdims=True)
        y = xc * lax.rsqrt(var + EPS) * lg_ref[...] + lb_ref[...]
        cact[r0:r0 + CV_ROW_TILE, :] = (y * jax.nn.sigmoid(y)).astype(BF16)

    da = a_ref.shape[2]
    h1 = (x_ref[0]
          + jnp.dot(a_ref[0], wo_ref[0:da, :], preferred_element_type=F32)
          + jnp.dot(cact[...], wo_ref[da:, :], preferred_element_type=F32))

    hn = _rms(h1, ng_ref[...]).astype(BF16)
    qx = jnp.dot(hn, wq_ref[...], preferred_element_type=F32).astype(BF16)
    hd = d // N_X_HEADS
    heads = []
    for h in range(N_X_HEADS):
        sl = slice(h * hd, (h + 1) * hd)
        s = lax.dot_general(qx[:, sl], mk_ref[0, :, sl], (((1,), (1,)), ((), ())),
                            preferred_element_type=F32)
        s = s - jnp.max(s, axis=-1, keepdims=True)
        e = jnp.exp2(s)
        p = e / jnp.sum(e, axis=-1, keepdims=True)
        heads.append(jnp.dot(p.astype(BF16), mv_ref[0, :, sl], preferred_element_type=F32))
    o = jnp.concatenate(heads, axis=-1).astype(BF16)
    o_ref[0] = h1 + jnp.dot(o, wco_ref[...], preferred_element_type=F32)


def _mixcross(x, a, c, cv_w, cv_b, ln_g, ln_b, w_out, ng, w_cq, mk, mv, w_co):
    b, s, d = x.shape
    t = min(TOK_BLOCK, s)
    da, cw, m = a.shape[2], c.shape[2], mk.shape[1]
    hpb = t // CV_HALO
    const = lambda shape: pl.BlockSpec(shape, lambda bi, i: (0,) * len(shape))
    tok = lambda w: pl.BlockSpec((1, t, w), lambda bi, i: (bi, i, 0))
    return pl.pallas_call(
        _mixcross_kernel,
        grid=(b, s // t),
        in_specs=[tok(d), tok(da), tok(cw),
                  pl.BlockSpec((1, CV_HALO, cw), lambda bi, i: (bi, jnp.maximum(i * hpb - 1, 0), 0)),
                  const((CV_KERNEL, SUBLANES, cw)), const((1, cw)), const((1, cw)), const((1, cw)),
                  const((da + cw, d)), const((1, d)), const((d, d)),
                  pl.BlockSpec((1, m, d), lambda bi, i: (bi, 0, 0)),
                  pl.BlockSpec((1, m, d), lambda bi, i: (bi, 0, 0)),
                  const((d, d))],
        out_specs=tok(d),
        out_shape=jax.ShapeDtypeStruct((b, s, d), F32),
        scratch_shapes=[pltpu.VMEM((t + CV_HALO, cw), F32),
                        pltpu.VMEM((SUBLANES - 1, t + CV_HALO, cw), F32),
                        pltpu.VMEM((t, cw), BF16)],
        compiler_params=pltpu.CompilerParams(
            dimension_semantics=("parallel", "parallel"), vmem_limit_bytes=VMEM_LIMIT_BYTES),
        name="mixcross",
    )(x, a, c, c, jnp.broadcast_to(cv_w[:, None, :], (CV_KERNEL, SUBLANES, cw)),
      cv_b.reshape(1, cw), ln_g.reshape(1, cw), ln_b.reshape(1, cw),
      w_out.astype(BF16), ng.reshape(1, d), w_cq.astype(BF16), mk, mv, w_co.astype(BF16))


def _convffn_kernel(h_ref, ng_ref, wu_ref, cg_ref, cv_ref, wd_ref, fg_ref, o_ref,
                    hn_scr, acc_scr, ga, va, gb, vb, gcarry, vcarry):
    t = h_ref.shape[1]
    i = pl.program_id(1)
    nchunk = wd_ref.shape[0]
    pad = SUBLANES

    @pl.when(i == 0)
    def _():
        gcarry[...] = jnp.zeros(gcarry.shape, F32)
        vcarry[...] = jnp.zeros(vcarry.shape, F32)

    hn_scr[...] = _rms(h_ref[0], ng_ref[...]).astype(BF16)
    acc_scr[...] = jnp.zeros(acc_scr.shape, F32)

    def up(f, gdst, vdst):
        hn = hn_scr[...]
        fc = gdst.shape[1]
        dff = nchunk * fc
        gdst[pad:, :] = jnp.dot(hn, wu_ref[:, pl.ds(pl.multiple_of(f * fc, fc), fc)],
                                preferred_element_type=F32)
        vdst[pad:, :] = jnp.dot(hn, wu_ref[:, pl.ds(pl.multiple_of(dff + f * fc, fc), fc)],
                                preferred_element_type=F32)

    def conv(sbuf, w_ref, f, carry):
        sbuf[0:pad, :] = carry[f]
        carry[f] = sbuf[t:t + pad, :]
        w = w_ref[f]
        out = w[FFN_KERNEL - 1:FFN_KERNEL, :] * sbuf[pad:, :]
        for j in range(FFN_KERNEL - 1):
            off = pad - (FFN_KERNEL - 1) + j
            out = out + w[j:j + 1, :] * sbuf[off:off + t, :]
        return out

    def down(f, gsrc, vsrc):
        g = conv(gsrc, cg_ref, f, gcarry)
        v = conv(vsrc, cv_ref, f, vcarry)
        z = (g * jax.nn.sigmoid(g) * v).astype(BF16)
        acc_scr[...] += jnp.dot(z, wd_ref[f], preferred_element_type=F32)

    up(0, ga, va)

    def pair(k, carry):
        f = 2 * k
        up(f + 1, gb, vb)
        down(f, ga, va)
        up(f + 2, ga, va)
        down(f + 1, gb, vb)
        return carry

    lax.fori_loop(0, nchunk // 2, pair, 0)
    down(nchunk - 1, ga, va)
    o_ref[0] = _rms(h_ref[0] + acc_scr[...], fg_ref[...])


def _convffn(h, ng, w_up, dw_w, w_down, fg):
    b, s, d = h.shape
    t = min(TOK_BLOCK, s)
    dff = w_down.shape[0]
    fc = FFN_CHUNK
    nchunk = dff // fc
    assert dff % fc == 0 and nchunk % 2 == 1
    dw = dw_w.reshape(FFN_KERNEL, 2, nchunk, fc).transpose(1, 2, 0, 3)
    wd = w_down.astype(BF16).reshape(nchunk, fc, d)
    const = lambda shape: pl.BlockSpec(shape, lambda bi, i: (0,) * len(shape))
    tok = pl.BlockSpec((1, t, d), lambda bi, i: (bi, i, 0))
    return pl.pallas_call(
        _convffn_kernel,
        grid=(b, s // t),
        in_specs=[tok, const((1, d)), const((d, 2 * dff)),
                  const((nchunk, FFN_KERNEL, fc)), const((nchunk, FFN_KERNEL, fc)),
                  const((nchunk, fc, d)), const((1, d))],
        out_specs=tok,
        out_shape=jax.ShapeDtypeStruct((b, s, d), F32),
        scratch_shapes=[pltpu.VMEM((t, d), BF16), pltpu.VMEM((t, d), F32),
                        pltpu.VMEM((t + SUBLANES, fc), F32), pltpu.VMEM((t + SUBLANES, fc), F32),
                        pltpu.VMEM((t + SUBLANES, fc), F32), pltpu.VMEM((t + SUBLANES, fc), F32),
                        pltpu.VMEM((nchunk, SUBLANES, fc), F32),
                        pltpu.VMEM((nchunk, SUBLANES, fc), F32)],
        compiler_params=pltpu.CompilerParams(
            dimension_semantics=("parallel", "arbitrary"), vmem_limit_bytes=VMEM_LIMIT_BYTES),
        name="convffn",
    )(h, ng.reshape(1, d), w_up.astype(BF16), dw[0], dw[1], wd, fg.reshape(1, d))


def kernel(x, mem, positions, norm_mix_g, w_in, lam_q1, lam_k1, lam_q2, lam_k2, subln_g, cv_dw_w, cv_dw_b, cv_ln_g, cv_ln_b, w_out, norm_cross_g, norm_mem_g, w_cq, w_ckv, w_co, norm_ffn_g, w_up, ffn_dw_w, w_down, norm_final_g):
    assert w_in.shape[0] == 1, "single-layer operation"
    q, k, v, c = _inproj(x, positions, norm_mix_g[0], w_in[0])
    mk, mv = _memkv(mem, norm_mem_g[0], w_ckv[0])
    a = _diff_attention(q, k, v, lam_q1[0], lam_k1[0], lam_q2[0], lam_k2[0], subln_g[0])
    h2 = _mixcross(x, a, c, cv_dw_w[0], cv_dw_b[0], cv_ln_g[0], cv_ln_b[0], w_out[0],
                   norm_cross_g[0], w_cq[0], mk, mv, w_co[0])
    return _convffn(h2, norm_ffn_g[0], w_up[0], ffn_dw_w[0], w_down[0], norm_final_g)
```

```python
import functools
import math

import jax
import jax.numpy as jnp
from jax import lax
from jax.experimental import pallas as pl
from jax.experimental.pallas import tpu as pltpu

F32 = jnp.float32
BF16 = jnp.bfloat16

N_DA_HEADS = 4
DA_VDIM = 128
DA_QKDIM = 64
ROT_DIM = 16
ROPE_THETA = 500000.0
CV_KERNEL = 31
N_X_HEADS = 4
FFN_KERNEL = 3
EPS = 1e-6
LAM_INIT = 0.8 - 0.6 * math.exp(-0.3 * 0)

LANES = 128
SUBLANES = 8
VMEM_LIMIT_BYTES = 56 * 1024 * 1024

NEG_BIG = -1e30

TOK_BLOCK = 512
ATT_UNROLLS = (8, 4, 2)
ATT_BLOCK = 512
CV_HALO = 32
CV_ROW_TILE = 32
FFN_GROUPS = (3, 3, 3, 2)
FFN_CHUNK = 256


def _rms(x, g):
    ms = jnp.mean(x * x, axis=-1, keepdims=True)
    return x * lax.rsqrt(ms + EPS) * g


def _inproj_kernel(x_ref, pos_ref, g_ref, w_ref, freq_ref, sel_ref, one_ref,
                   q_ref, k_ref, v_ref, c_ref):
    da = N_DA_HEADS * DA_VDIM
    xn = _rms(x_ref[0], g_ref[...]).astype(BF16)
    ang = pos_ref[0].astype(F32) * freq_ref[...]

    def split3(v):
        v1 = v.astype(BF16)
        r1 = v - v1.astype(F32)
        v2 = r1.astype(BF16)
        v3 = (r1 - v2.astype(F32)).astype(BF16)
        return [v1, v2, v3]

    trig = jnp.concatenate(split3(jnp.cos(ang)) + split3(jnp.sin(ang)), axis=0)
    tab = lax.dot_general(trig, sel_ref[...], (((0,), (0,)), ((), ())),
                          preferred_element_type=F32)
    cos = tab[:, 0:LANES] + one_ref[...]
    sa = tab[:, LANES:2 * LANES]
    sb = tab[:, 2 * LANES:3 * LANES]

    def rope(t):
        return (t * cos + pltpu.roll(t, ROT_DIM // 2, 1) * sa
                + pltpu.roll(t, LANES - ROT_DIM // 2, 1) * sb)

    cw = (w_ref.shape[1] - 3 * da) // 2
    uv = jnp.dot(xn, w_ref[:, 3 * da:3 * da + cw], preferred_element_type=F32)
    ug = jnp.dot(xn, w_ref[:, 3 * da + cw:], preferred_element_type=F32)
    c_ref[0] = uv * jax.nn.sigmoid(ug)
    qp = jnp.dot(xn, w_ref[:, 0:da], preferred_element_type=F32)
    kp = jnp.dot(xn, w_ref[:, da:2 * da], preferred_element_type=F32)
    scale = math.log2(math.e) / math.sqrt(DA_QKDIM)
    for h in range(N_DA_HEADS):
        sl = slice(h * LANES, (h + 1) * LANES)
        q_ref[0, :, sl] = (rope(qp[:, sl]) * scale).astype(BF16)
        k_ref[0, :, sl] = rope(kp[:, sl]).astype(BF16)
    v_ref[0] = jnp.dot(xn, w_ref[:, 2 * da:3 * da], preferred_element_type=F32).astype(BF16)


def _inproj(x, positions, g, w_in):
    b, s, d = x.shape
    t = min(TOK_BLOCK, s)
    da = N_DA_HEADS * DA_VDIM
    cw = (w_in.shape[1] - 3 * da) // 2
    half = ROT_DIM // 2
    inv_freq = ROPE_THETA ** (-jnp.arange(0, ROT_DIM, 2, dtype=F32) / ROT_DIM)
    freq = jnp.broadcast_to(inv_freq[:, None], (half, t))
    j = jnp.arange(LANES) % DA_QKDIM
    hit = (jnp.arange(half)[:, None] == (j % half)[None, :]) & (j < ROT_DIM)[None, :]
    e_cos = hit.astype(F32)
    e_sa = (hit & (j >= half)[None, :]).astype(F32)
    e_sb = -(hit & (j < half)[None, :]).astype(F32)
    zero = jnp.zeros_like(e_cos)
    sel = jnp.concatenate([jnp.concatenate([e_cos, zero, zero], axis=1)] * 3
                          + [jnp.concatenate([zero, e_sa, e_sb], axis=1)] * 3, axis=0).astype(BF16)
    one = jnp.where(j < ROT_DIM, 0.0, 1.0).astype(F32)[None, :]
    const = lambda shape: pl.BlockSpec(shape, lambda bi, i: (0,) * len(shape))
    tok = lambda w: pl.BlockSpec((1, t, w), lambda bi, i: (bi, i, 0))
    return pl.pallas_call(
        _inproj_kernel,
        grid=(b, s // t),
        in_specs=[tok(d), pl.BlockSpec((1, 1, t), lambda bi, i: (bi * (s // t) + i, 0, 0)),
                  const((1, d)), const(w_in.shape),
                  const((half, t)), const((6 * half, 3 * LANES)), const((1, LANES))],
        out_specs=[tok(da), tok(da), tok(da), tok(cw)],
        out_shape=[jax.ShapeDtypeStruct((b, s, da), BF16)] * 3
                  + [jax.ShapeDtypeStruct((b, s, cw), F32)],
        compiler_params=pltpu.CompilerParams(
            dimension_semantics=("parallel", "parallel"), vmem_limit_bytes=VMEM_LIMIT_BYTES),
        name="inproj",
    )(x, positions.reshape(b * (s // t), 1, t), g.reshape(1, d), w_in.astype(BF16), freq, sel, one)


def _memkv_kernel(m_ref, g_ref, w_ref, k_ref, v_ref):
    d = m_ref.shape[2]
    mn = _rms(m_ref[0], g_ref[...]).astype(BF16)
    kv = jnp.dot(mn, w_ref[...], preferred_element_type=F32)
    scale = math.log2(math.e) / math.sqrt(d // N_X_HEADS)
    k_ref[0] = (kv[:, :d] * scale).astype(BF16)
    v_ref[0] = kv[:, d:].astype(BF16)


def _memkv(mem, g, w_ckv):
    b, m, d = mem.shape
    return pl.pallas_call(
        _memkv_kernel,
        grid=(b,),
        in_specs=[pl.BlockSpec((1, m, d), lambda bi: (bi, 0, 0)),
                  pl.BlockSpec((1, d), lambda bi: (0, 0)),
                  pl.BlockSpec((d, 2 * d), lambda bi: (0, 0))],
        out_specs=[pl.BlockSpec((1, m, d), lambda bi: (bi, 0, 0))] * 2,
        out_shape=[jax.ShapeDtypeStruct((b, m, d), BF16)] * 2,
        compiler_params=pltpu.CompilerParams(
            dimension_semantics=("parallel",), vmem_limit_bytes=VMEM_LIMIT_BYTES),
        name="memkv",
    )(mem, g.reshape(1, d), w_ckv.astype(BF16))


def _attn_kernel(q_ref, qn_ref, k_ref, v_ref, lq1_ref, lk1_ref, lq2_ref, lk2_ref, g_ref, o_ref,
                 qs_scr, sa_scr, sb_scr, m_scr, acc_scr):
    t = ATT_BLOCK
    rows = 2 * t
    qi = pl.program_id(2)
    lane = lax.broadcasted_iota(jnp.int32, (t, LANES), 1)
    ones_col = jnp.where(lane == 0, 1.0, 0.0).astype(BF16)

    def stack_q(q):
        zero = jnp.zeros_like(q)
        qs_scr[0:t, :] = jnp.where(lane < DA_QKDIM, q, zero)
        qs_scr[t:rows, :] = jnp.where(lane >= DA_QKDIM, q, zero)

    m_scr[...] = jnp.full(m_scr.shape, NEG_BIG, F32)
    acc_scr[...] = jnp.zeros(acc_scr.shape, F32)

    def scores(j, dst, nsplit=1):
        k0 = pl.multiple_of(j * t, t)
        kc = k_ref[0, pl.ds(k0, t), :]
        rs = rows // nsplit
        for r in range(0, rows, rs):
            dst[r:r + rs, :] = lax.dot_general(qs_scr[r:r + rs, :], kc,
                                               (((1,), (1,)), ((), ())), preferred_element_type=F32)

    def consume(j, src, masked, nsplit=1):
        k0 = pl.multiple_of(j * t, t)
        vc = jnp.concatenate([v_ref[0, pl.ds(k0, t), :], ones_col], axis=1)
        s = src[...]
        if masked:
            row = lax.broadcasted_iota(jnp.int32, (rows, t), 0) % t
            col = lax.broadcasted_iota(jnp.int32, (rows, t), 1)
            s = jnp.where(col <= row, s, NEG_BIG)
        m_old = m_scr[...]
        m_new = jnp.maximum(m_old, jnp.max(s, axis=1, keepdims=True))
        alpha = jnp.exp2(m_old - m_new)
        p = jnp.exp2(s - jnp.tile(m_new, (1, t // LANES))).astype(BF16)
        rs = rows // nsplit
        for r in range(0, rows, rs):
            acc_scr[r:r + rs, :] = (jnp.tile(alpha[r:r + rs], (1, 2)) * acc_scr[r:r + rs, :]
                                    + jnp.dot(p[r:r + rs], vc, preferred_element_type=F32))
        m_scr[...] = m_new

    @pl.when(qi == 0)
    def _():
        stack_q(q_ref[0])
        scores(0, sa_scr, nsplit=2)

    def run(j, nblk):
        for u in range(nblk):
            src, dst = (sa_scr, sb_scr) if u % 2 == 0 else (sb_scr, sa_scr)
            scores(j + u + 1, dst)
            consume(j + u, src, masked=False)

    done = 0
    for unroll in ATT_UNROLLS:
        def body(i, carry, unroll=unroll, done=done):
            run(done + unroll * i, unroll)
            return carry

        trips = (qi - done) // unroll
        lax.fori_loop(0, trips, body, 0)
        done = done + trips * unroll

    @pl.when(qi % 2 == 1)
    def _():
        scores(qi, sb_scr)
        consume(qi - 1, sa_scr, masked=False)
        consume(qi, sb_scr, masked=True, nsplit=2)

    @pl.when(qi % 2 == 0)
    def _():
        consume(qi, sa_scr, masked=True, nsplit=2)

    lam = (jnp.exp(jnp.sum(lq1_ref[...] * lk1_ref[...], keepdims=True))
           - jnp.exp(jnp.sum(lq2_ref[...] * lk2_ref[...], keepdims=True)) + LAM_INIT)
    acc = acc_scr[...]
    o = acc[:, :DA_VDIM] / acc[:, DA_VDIM:DA_VDIM + 1]
    a = o[:t] - lam * o[t:]
    o_ref[0] = (_rms(a, g_ref[...]) * (1.0 - LAM_INIT)).astype(o_ref.dtype)

    stack_q(qn_ref[0])
    scores(0, sa_scr, nsplit=2)


def _diff_attention(q, k, v, lq1, lk1, lq2, lk2, subln_g):
    b, s, _ = q.shape
    tq = ATT_BLOCK
    assert s % tq == 0
    nq = s // tq
    vec = lambda n: pl.BlockSpec((1, n), lambda bi, h, i: (0, 0))
    return pl.pallas_call(
        _attn_kernel,
        grid=(b, N_DA_HEADS, s // tq),
        in_specs=[pl.BlockSpec((1, tq, LANES), lambda bi, h, i: (bi, i, h)),
                  pl.BlockSpec((1, tq, LANES), lambda bi, h, i: (bi, jnp.minimum(i + 1, nq - 1), h)),
                  pl.BlockSpec((1, s, LANES), lambda bi, h, i: (bi, 0, h)),
                  pl.BlockSpec((1, s, DA_VDIM), lambda bi, h, i: (bi, 0, h)),
                  vec(DA_QKDIM), vec(DA_QKDIM), vec(DA_QKDIM), vec(DA_QKDIM), vec(DA_VDIM)],
        out_specs=pl.BlockSpec((1, tq, DA_VDIM), lambda bi, h, i: (bi, i, h)),
        out_shape=jax.ShapeDtypeStruct((b, s, N_DA_HEADS * DA_VDIM), BF16),
        scratch_shapes=[pltpu.VMEM((2 * tq, LANES), BF16),
                        pltpu.VMEM((2 * tq, tq), F32), pltpu.VMEM((2 * tq, tq), F32),
                        pltpu.VMEM((2 * tq, LANES), F32),
                        pltpu.VMEM((2 * tq, 2 * DA_VDIM), F32)],
        compiler_params=pltpu.CompilerParams(
            dimension_semantics=("parallel", "parallel", "arbitrary"),
            vmem_limit_bytes=VMEM_LIMIT_BYTES),
        name="diffattn",
    )(q, q, k, v, lq1.reshape(1, -1), lk1.reshape(1, -1), lq2.reshape(1, -1), lk2.reshape(1, -1),
      subln_g.reshape(1, -1))


def _mixcross_kernel(x_ref, a_ref, c_ref, halo_ref, cw_ref, cb_ref, lg_ref, lb_ref, wo_ref,
                     ng_ref, wq_ref, mk_ref, mv_ref, wco_ref, o_ref, buf, shf, cact):
    t = x_ref.shape[1]
    d = x_ref.shape[2]
    i = pl.program_id(1)
    n = t + CV_HALO
    halo = halo_ref[0]
    buf[0:CV_HALO, :] = jnp.where(i > 0, halo, jnp.zeros_like(halo))
    buf[CV_HALO:, :] = c_ref[0]
    for p in range(1, SUBLANES):
        shf[p - 1, SUBLANES:n, :] = buf[SUBLANES - p:n - p, :]

    for r0 in range(0, t, CV_ROW_TILE):
        acc = jnp.zeros((CV_ROW_TILE, buf.shape[1]), F32) + cb_ref[...]
        for dl in range(CV_KERNEL):
            al, p = divmod(dl, SUBLANES)
            row = r0 + CV_HALO - al * SUBLANES
            src = buf if p == 0 else shf.at[p - 1]
            wj = CV_KERNEL - 1 - dl
            wt = jnp.tile(cw_ref[wj], (CV_ROW_TILE // SUBLANES, 1))
            acc = acc + wt * src[row:row + CV_ROW_TILE, :]
        mu = jnp.mean(acc, axis=-1, keepdims=True)
        xc = acc - mu
        var = jnp.mean(xc * xc, axis=-1, keepdims=True)
        y = xc * lax.rsqrt(var + EPS) * lg_ref[...] + lb_ref[...]
        cact[r0:r0 + CV_ROW_TILE, :] = (y * jax.nn.sigmoid(y)).astype(BF16)

    da = a_ref.shape[2]
    h1 = (x_ref[0]
          + jnp.dot(a_ref[0], wo_ref[0:da, :], preferred_element_type=F32)
          + jnp.dot(cact[...], wo_ref[da:, :], preferred_element_type=F32))

    hn = _rms(h1, ng_ref[...]).astype(BF16)
    qx = jnp.dot(hn, wq_ref[...], preferred_element_type=F32).astype(BF16)
    hd = d // N_X_HEADS
    heads = []
    for h in range(N_X_HEADS):
        sl = slice(h * hd, (h + 1) * hd)
        s = lax.dot_general(qx[:, sl], mk_ref[0, :, sl], (((1,), (1,)), ((), ())),
                            preferred_element_type=F32)
        s = s - jnp.max(s, axis=-1, keepdims=True)
        e = jnp.exp2(s)
        p = e / jnp.sum(e, axis=-1, keepdims=True)
        heads.append(jnp.dot(p.astype(BF16), mv_ref[0, :, sl], preferred_element_type=F32))
    o = jnp.concatenate(heads, axis=-1).astype(BF16)
    o_ref[0] = h1 + jnp.dot(o, wco_ref[...], preferred_element_type=F32)


def _mixcross(x, a, c, cv_w, cv_b, ln_g, ln_b, w_out, ng, w_cq, mk, mv, w_co):
    b, s, d = x.shape
    t = min(TOK_BLOCK, s)
    da, cw, m = a.shape[2], c.shape[2], mk.shape[1]
    hpb = t // CV_HALO
    const = lambda shape: pl.BlockSpec(shape, lambda bi, i: (0,) * len(shape))
    tok = lambda w: pl.BlockSpec((1, t, w), lambda bi, i: (bi, i, 0))
    return pl.pallas_call(
        _mixcross_kernel,
        grid=(b, s // t),
        in_specs=[tok(d), tok(da), tok(cw),
                  pl.BlockSpec((1, CV_HALO, cw), lambda bi, i: (bi, jnp.maximum(i * hpb - 1, 0), 0)),
                  const((CV_KERNEL, SUBLANES, cw)), const((1, cw)), const((1, cw)), const((1, cw)),
                  const((da + cw, d)), const((1, d)), const((d, d)),
                  pl.BlockSpec((1, m, d), lambda bi, i: (bi, 0, 0)),
                  pl.BlockSpec((1, m, d), lambda bi, i: (bi, 0, 0)),
                  const((d, d))],
        out_specs=tok(d),
        out_shape=jax.ShapeDtypeStruct((b, s, d), F32),
        scratch_shapes=[pltpu.VMEM((t + CV_HALO, cw), F32),
                        pltpu.VMEM((SUBLANES - 1, t + CV_HALO, cw), F32),
                        pltpu.VMEM((t, cw), BF16)],
        compiler_params=pltpu.CompilerParams(
            dimension_semantics=("parallel", "parallel"), vmem_limit_bytes=VMEM_LIMIT_BYTES),
        name="mixcross",
    )(x, a, c, c, jnp.broadcast_to(cv_w[:, None, :], (CV_KERNEL, SUBLANES, cw)),
      cv_b.reshape(1, cw), ln_g.reshape(1, cw), ln_b.reshape(1, cw),
      w_out.astype(BF16), ng.reshape(1, d), w_cq.astype(BF16), mk, mv, w_co.astype(BF16))


def _convffn_kernel(h_ref, ng_ref, wu_ref, cg_ref, cv_ref, wd_ref, fg_ref, o_ref,
                    hn_scr, acc_scr, ga, va, gb, vb, gcarry, vcarry):
    t = h_ref.shape[1]
    i = pl.program_id(1)
    nchunk = wd_ref.shape[0]
    pad = SUBLANES

    @pl.when(i == 0)
    def _():
        gcarry[...] = jnp.zeros(gcarry.shape, F32)
        vcarry[...] = jnp.zeros(vcarry.shape, F32)

    hn_scr[...] = _rms(h_ref[0], ng_ref[...]).astype(BF16)

    def up(f, gdst, vdst):
        hn = hn_scr[...]
        fc = gdst.shape[1]
        dff = nchunk * fc
        gdst[pad:, :] = jnp.dot(hn, wu_ref[:, f * fc:(f + 1) * fc], preferred_element_type=F32)
        vdst[pad:, :] = jnp.dot(hn, wu_ref[:, dff + f * fc:dff + (f + 1) * fc],
                                preferred_element_type=F32)

    def conv(sbuf, w_ref, f, carry):
        sbuf[0:pad, :] = carry[f]
        carry[f] = sbuf[t:t + pad, :]
        w = w_ref[f]
        out = w[FFN_KERNEL - 1:FFN_KERNEL, :] * sbuf[pad:, :]
        for j in range(FFN_KERNEL - 1):
            off = pad - (FFN_KERNEL - 1) + j
            out = out + w[j:j + 1, :] * sbuf[off:off + t, :]
        return out

    def gate(f, gsrc, vsrc):
        g = conv(gsrc, cg_ref, f, gcarry)
        v = conv(vsrc, cv_ref, f, vcarry)
        return (g * jax.nn.sigmoid(g) * v).astype(BF16)

    up(0, ga, va)
    bufs = ((ga, va), (gb, vb))
    f = 0
    for group in FFN_GROUPS:
        total = None
        for u in range(group):
            if f + 1 < nchunk:
                up(f + 1, *bufs[(f + 1) % 2])
            part = jnp.dot(gate(f, *bufs[f % 2]), wd_ref[f], preferred_element_type=F32)
            total = part if total is None else total + part
            f += 1
        if f == group:
            acc_scr[...] = total
        elif f < nchunk:
            acc_scr[...] += total
    o_ref[0] = _rms(h_ref[0] + acc_scr[...] + total, fg_ref[...])


def _convffn(h, ng, w_up, dw_w, w_down, fg):
    b, s, d = h.shape
    t = min(TOK_BLOCK, s)
    dff = w_down.shape[0]
    fc = FFN_CHUNK
    nchunk = dff // fc
    assert dff % fc == 0 and sum(FFN_GROUPS) == nchunk
    dw = dw_w.reshape(FFN_KERNEL, 2, nchunk, fc).transpose(1, 2, 0, 3)
    wd = w_down.astype(BF16).reshape(nchunk, fc, d)
    const = lambda shape: pl.BlockSpec(shape, lambda bi, i: (0,) * len(shape))
    tok = pl.BlockSpec((1, t, d), lambda bi, i: (bi, i, 0))
    return pl.pallas_call(
        _convffn_kernel,
        grid=(b, s // t),
        in_specs=[tok, const((1, d)), const((d, 2 * dff)),
                  const((nchunk, FFN_KERNEL, fc)), const((nchunk, FFN_KERNEL, fc)),
                  const((nchunk, fc, d)), const((1, d))],
        out_specs=tok,
        out_shape=jax.ShapeDtypeStruct((b, s, d), F32),
        scratch_shapes=[pltpu.VMEM((t, d), BF16), pltpu.VMEM((t, d), F32),
                        pltpu.VMEM((t + SUBLANES, fc), F32), pltpu.VMEM((t + SUBLANES, fc), F32),
                        pltpu.VMEM((t + SUBLANES, fc), F32), pltpu.VMEM((t + SUBLANES, fc), F32),
                        pltpu.VMEM((nchunk, SUBLANES, fc), F32),
                        pltpu.VMEM((nchunk, SUBLANES, fc), F32)],
        compiler_params=pltpu.CompilerParams(
            dimension_semantics=("parallel", "arbitrary"), vmem_limit_bytes=VMEM_LIMIT_BYTES),
        name="convffn",
    )(h, ng.reshape(1, d), w_up.astype(BF16), dw[0], dw[1], wd, fg.reshape(1, d))


def kernel(x, mem, positions, norm_mix_g, w_in, lam_q1, lam_k1, lam_q2, lam_k2, subln_g, cv_dw_w, cv_dw_b, cv_ln_g, cv_ln_b, w_out, norm_cross_g, norm_mem_g, w_cq, w_ckv, w_co, norm_ffn_g, w_up, ffn_dw_w, w_down, norm_final_g):
    assert w_in.shape[0] == 1, "single-layer operation"
    q, k, v, c = _inproj(x, positions, norm_mix_g[0], w_in[0])
    mk, mv = _memkv(mem, norm_mem_g[0], w_ckv[0])
    a = _diff_attention(q, k, v, lam_q1[0], lam_k1[0], lam_q2[0], lam_k2[0], subln_g[0])
    h2 = _mixcross(x, a, c, cv_dw_w[0], cv_dw_b[0], cv_ln_g[0], cv_ln_b[0], w_out[0],
                   norm_cross_g[0], w_cq[0], mk, mv, w_co[0])
    return _convffn(h2, norm_ffn_g[0], w_up[0], ffn_dw_w[0], w_down[0], norm_final_g)
```

```python
import functools
import math

import jax
import jax.numpy as jnp
from jax import lax
from jax.experimental import pallas as pl
from jax.experimental.pallas import tpu as pltpu

F32 = jnp.float32
BF16 = jnp.bfloat16

N_DA_HEADS = 4
DA_VDIM = 128
DA_QKDIM = 64
ROT_DIM = 16
ROPE_THETA = 500000.0
CV_KERNEL = 31
N_X_HEADS = 4
FFN_KERNEL = 3
EPS = 1e-6
LAM_INIT = 0.8 - 0.6 * math.exp(-0.3 * 0)

LANES = 128
SUBLANES = 8
VMEM_LIMIT_BYTES = 56 * 1024 * 1024

NEG_BIG = -1e30

TOK_BLOCK = 512
ATT_UNROLLS = (16, 8, 4, 2)
ATT_BLOCK = 512
CV_HALO = 32
CV_ROW_TILE = 32
FFN_GROUPS = (3, 3, 3, 2)
FFN_CHUNK = 256


def _rms(x, g):
    ms = jnp.mean(x * x, axis=-1, keepdims=True)
    return x * lax.rsqrt(ms + EPS) * g


def _inproj_kernel(x_ref, pos_ref, g_ref, w_ref, freq_ref, sel_ref, one_ref,
                   q_ref, k_ref, v_ref, c_ref):
    da = N_DA_HEADS * DA_VDIM
    xn = _rms(x_ref[0], g_ref[...]).astype(BF16)
    ang = pos_ref[0].astype(F32) * freq_ref[...]

    def split3(v):
        v1 = v.astype(BF16)
        r1 = v - v1.astype(F32)
        v2 = r1.astype(BF16)
        v3 = (r1 - v2.astype(F32)).astype(BF16)
        return [v1, v2, v3]

    trig = jnp.concatenate(split3(jnp.cos(ang)) + split3(jnp.sin(ang)), axis=0)
    tab = lax.dot_general(trig, sel_ref[...], (((0,), (0,)), ((), ())),
                          preferred_element_type=F32)
    cos = tab[:, 0:LANES] + one_ref[...]
    sa = tab[:, LANES:2 * LANES]
    sb = tab[:, 2 * LANES:3 * LANES]

    def rope(t):
        return (t * cos + pltpu.roll(t, ROT_DIM // 2, 1) * sa
                + pltpu.roll(t, LANES - ROT_DIM // 2, 1) * sb)

    cw = (w_ref.shape[1] - 3 * da) // 2
    uv = jnp.dot(xn, w_ref[:, 3 * da:3 * da + cw], preferred_element_type=F32)
    ug = jnp.dot(xn, w_ref[:, 3 * da + cw:], preferred_element_type=F32)
    c_ref[0] = uv * jax.nn.sigmoid(ug)
    qp = jnp.dot(xn, w_ref[:, 0:da], preferred_element_type=F32)
    kp = jnp.dot(xn, w_ref[:, da:2 * da], preferred_element_type=F32)
    scale = math.log2(math.e) / math.sqrt(DA_QKDIM)
    for h in range(N_DA_HEADS):
        sl = slice(h * LANES, (h + 1) * LANES)
        q_ref[0, :, sl] = (rope(qp[:, sl]) * scale).astype(BF16)
        k_ref[0, :, sl] = rope(kp[:, sl]).astype(BF16)
    v_ref[0] = jnp.dot(xn, w_ref[:, 2 * da:3 * da], preferred_element_type=F32).astype(BF16)


def _inproj(x, positions, g, w_in):
    b, s, d = x.shape
    t = min(TOK_BLOCK, s)
    da = N_DA_HEADS * DA_VDIM
    cw = (w_in.shape[1] - 3 * da) // 2
    half = ROT_DIM // 2
    inv_freq = ROPE_THETA ** (-jnp.arange(0, ROT_DIM, 2, dtype=F32) / ROT_DIM)
    freq = jnp.broadcast_to(inv_freq[:, None], (half, t))
    j = jnp.arange(LANES) % DA_QKDIM
    hit = (jnp.arange(half)[:, None] == (j % half)[None, :]) & (j < ROT_DIM)[None, :]
    e_cos = hit.astype(F32)
    e_sa = (hit & (j >= half)[None, :]).astype(F32)
    e_sb = -(hit & (j < half)[None, :]).astype(F32)
    zero = jnp.zeros_like(e_cos)
    sel = jnp.concatenate([jnp.concatenate([e_cos, zero, zero], axis=1)] * 3
                          + [jnp.concatenate([zero, e_sa, e_sb], axis=1)] * 3, axis=0).astype(BF16)
    one = jnp.where(j < ROT_DIM, 0.0, 1.0).astype(F32)[None, :]
    const = lambda shape: pl.BlockSpec(shape, lambda bi, i: (0,) * len(shape))
    tok = lambda w: pl.BlockSpec((1, t, w), lambda bi, i: (bi, i, 0))
    return pl.pallas_call(
        _inproj_kernel,
        grid=(b, s // t),
        in_specs=[tok(d), pl.BlockSpec((1, 1, t), lambda bi, i: (bi * (s // t) + i, 0, 0)),
                  const((1, d)), const(w_in.shape),
                  const((half, t)), const((6 * half, 3 * LANES)), const((1, LANES))],
        out_specs=[tok(da), tok(da), tok(da), tok(cw)],
        out_shape=[jax.ShapeDtypeStruct((b, s, da), BF16)] * 3
                  + [jax.ShapeDtypeStruct((b, s, cw), F32)],
        compiler_params=pltpu.CompilerParams(
            dimension_semantics=("parallel", "parallel"), vmem_limit_bytes=VMEM_LIMIT_BYTES),
        name="inproj",
    )(x, positions.reshape(b * (s // t), 1, t), g.reshape(1, d), w_in.astype(BF16), freq, sel, one)


def _memkv_kernel(m_ref, g_ref, w_ref, k_ref, v_ref):
    d = m_ref.shape[2]
    mn = _rms(m_ref[0], g_ref[...]).astype(BF16)
    kv = jnp.dot(mn, w_ref[...], preferred_element_type=F32)
    scale = math.log2(math.e) / math.sqrt(d // N_X_HEADS)
    k_ref[0] = (kv[:, :d] * scale).astype(BF16)
    v_ref[0] = kv[:, d:].astype(BF16)


def _memkv(mem, g, w_ckv):
    b, m, d = mem.shape
    return pl.pallas_call(
        _memkv_kernel,
        grid=(b,),
        in_specs=[pl.BlockSpec((1, m, d), lambda bi: (bi, 0, 0)),
                  pl.BlockSpec((1, d), lambda bi: (0, 0)),
                  pl.BlockSpec((d, 2 * d), lambda bi: (0, 0))],
        out_specs=[pl.BlockSpec((1, m, d), lambda bi: (bi, 0, 0))] * 2,
        out_shape=[jax.ShapeDtypeStruct((b, m, d), BF16)] * 2,
        compiler_params=pltpu.CompilerParams(
            dimension_semantics=("parallel",), vmem_limit_bytes=VMEM_LIMIT_BYTES),
        name="memkv",
    )(mem, g.reshape(1, d), w_ckv.astype(BF16))


def _attn_kernel(q_ref, qn_ref, k_ref, v_ref, lq1_ref, lk1_ref, lq2_ref, lk2_ref, g_ref, o_ref,
                 qs_scr, sa_scr, sb_scr, m_scr, acc_scr):
    t = ATT_BLOCK
    rows = 2 * t
    qi = pl.program_id(2)
    lane = lax.broadcasted_iota(jnp.int32, (t, LANES), 1)
    ones_col = jnp.where(lane == 0, 1.0, 0.0).astype(BF16)

    def stack_q(q):
        zero = jnp.zeros_like(q)
        qs_scr[0:t, :] = jnp.where(lane < DA_QKDIM, q, zero)
        qs_scr[t:rows, :] = jnp.where(lane >= DA_QKDIM, q, zero)

    def reset_state():
        m_scr[...] = jnp.full(m_scr.shape, NEG_BIG, F32)
        acc_scr[...] = jnp.zeros(acc_scr.shape, F32)

    def scores(j, dst, nsplit=1):
        k0 = pl.multiple_of(j * t, t)
        kc = k_ref[0, pl.ds(k0, t), :]
        rs = rows // nsplit
        for r in range(0, rows, rs):
            dst[r:r + rs, :] = lax.dot_general(qs_scr[r:r + rs, :], kc,
                                               (((1,), (1,)), ((), ())), preferred_element_type=F32)

    def consume(j, src, masked, nsplit=1):
        k0 = pl.multiple_of(j * t, t)
        vc = jnp.concatenate([v_ref[0, pl.ds(k0, t), :], ones_col], axis=1)
        s = src[...]
        if masked:
            row = lax.broadcasted_iota(jnp.int32, (rows, t), 0) % t
            col = lax.broadcasted_iota(jnp.int32, (rows, t), 1)
            s = jnp.where(col <= row, s, NEG_BIG)
        m_old = m_scr[...]
        m_new = jnp.maximum(m_old, jnp.max(s, axis=1, keepdims=True))
        alpha = jnp.exp2(m_old - m_new)
        p = jnp.exp2(s - jnp.tile(m_new, (1, t // LANES))).astype(BF16)
        rs = rows // nsplit
        for r in range(0, rows, rs):
            acc_scr[r:r + rs, :] = (jnp.tile(alpha[r:r + rs], (1, 2)) * acc_scr[r:r + rs, :]
                                    + jnp.dot(p[r:r + rs], vc, preferred_element_type=F32))
        m_scr[...] = m_new

    @pl.when(qi == 0)
    def _():
        reset_state()
        stack_q(q_ref[0])
        scores(0, sa_scr, nsplit=2)

    def run(j, nblk):
        for u in range(nblk):
            src, dst = (sa_scr, sb_scr) if u % 2 == 0 else (sb_scr, sa_scr)
            scores(j + u + 1, dst)
            consume(j + u, src, masked=False)

    done = 0
    for unroll in ATT_UNROLLS:
        def body(i, carry, unroll=unroll, done=done):
            run(done + unroll * i, unroll)
            return carry

        trips = (qi - done) // unroll
        lax.fori_loop(0, trips, body, 0)
        done = done + trips * unroll

    @pl.when(qi % 2 == 1)
    def _():
        scores(qi, sb_scr)
        consume(qi - 1, sa_scr, masked=False)
        consume(qi, sb_scr, masked=True, nsplit=2)

    @pl.when(qi % 2 == 0)
    def _():
        consume(qi, sa_scr, masked=True, nsplit=2)

    lam = (jnp.exp(jnp.sum(lq1_ref[...] * lk1_ref[...], keepdims=True))
           - jnp.exp(jnp.sum(lq2_ref[...] * lk2_ref[...], keepdims=True)) + LAM_INIT)
    acc = acc_scr[...]
    o = acc[:, :DA_VDIM] / acc[:, DA_VDIM:DA_VDIM + 1]
    a = o[:t] - lam * o[t:]
    o_ref[0] = (_rms(a, g_ref[...]) * (1.0 - LAM_INIT)).astype(o_ref.dtype)

    reset_state()
    stack_q(qn_ref[0])
    scores(0, sa_scr, nsplit=2)


def _diff_attention(q, k, v, lq1, lk1, lq2, lk2, subln_g):
    b, s, _ = q.shape
    tq = ATT_BLOCK
    assert s % tq == 0
    nq = s // tq
    vec = lambda n: pl.BlockSpec((1, n), lambda bi, h, i: (0, 0))
    return pl.pallas_call(
        _attn_kernel,
        grid=(b, N_DA_HEADS, s // tq),
        in_specs=[pl.BlockSpec((1, tq, LANES), lambda bi, h, i: (bi, i, h)),
                  pl.BlockSpec((1, tq, LANES), lambda bi, h, i: (bi, jnp.minimum(i + 1, nq - 1), h)),
                  pl.BlockSpec((1, s, LANES), lambda bi, h, i: (bi, 0, h)),
                  pl.BlockSpec((1, s, DA_VDIM), lambda bi, h, i: (bi, 0, h)),
                  vec(DA_QKDIM), vec(DA_QKDIM), vec(DA_QKDIM), vec(DA_QKDIM), vec(DA_VDIM)],
        out_specs=pl.BlockSpec((1, tq, DA_VDIM), lambda bi, h, i: (bi, i, h)),
        out_shape=jax.ShapeDtypeStruct((b, s, N_DA_HEADS * DA_VDIM), BF16),
        scratch_shapes=[pltpu.VMEM((2 * tq, LANES), BF16),
                        pltpu.VMEM((2 * tq, tq), F32), pltpu.VMEM((2 * tq, tq), F32),
                        pltpu.VMEM((2 * tq, LANES), F32),
                        pltpu.VMEM((2 * tq, 2 * DA_VDIM), F32)],
        compiler_params=pltpu.CompilerParams(
            dimension_semantics=("parallel", "parallel", "arbitrary"),
            vmem_limit_bytes=VMEM_LIMIT_BYTES),
        name="diffattn",
    )(q, q, k, v, lq1.reshape(1, -1), lk1.reshape(1, -1), lq2.reshape(1, -1), lk2.reshape(1, -1),
      subln_g.reshape(1, -1))


def _mixcross_kernel(x_ref, a_ref, c_ref, halo_ref, cw_ref, cb_ref, lg_ref, lb_ref, wo_ref,
                     ng_ref, wq_ref, mk_ref, mv_ref, wco_ref, o_ref, buf, shf, cact):
    t = x_ref.shape[1]
    d = x_ref.shape[2]
    i = pl.program_id(1)
    n = t + CV_HALO
    halo = halo_ref[0]
    buf[0:CV_HALO, :] = jnp.where(i > 0, halo, jnp.zeros_like(halo))
    buf[CV_HALO:, :] = c_ref[0]
    for p in range(1, SUBLANES):
        shf[p - 1, SUBLANES:n, :] = buf[SUBLANES - p:n - p, :]

    for r0 in range(0, t, CV_ROW_TILE):
        acc = jnp.zeros((CV_ROW_TILE, buf.shape[1]), F32) + cb_ref[...]
        for dl in range(CV_KERNEL):
            al, p = divmod(dl, SUBLANES)
            row = r0 + CV_HALO - al * SUBLANES
            src = buf if p == 0 else shf.at[p - 1]
            wj = CV_KERNEL - 1 - dl
            wt = jnp.tile(cw_ref[wj], (CV_ROW_TILE // SUBLANES, 1))
            acc = acc + wt * src[row:row + CV_ROW_TILE, :]
        mu = jnp.mean(acc, axis=-1, keepdims=True)
        xc = acc - mu
        var = jnp.mean(xc * xc, axis=-1, keepdims=True)
        y = xc * lax.rsqrt(var + EPS) * lg_ref[...] + lb_ref[...]
        cact[r0:r0 + CV_ROW_TILE, :] = (y * jax.nn.sigmoid(y)).astype(BF16)

    da = a_ref.shape[2]
    h1 = (x_ref[0]
          + jnp.dot(a_ref[0], wo_ref[0:da, :], preferred_element_type=F32)
          + jnp.dot(cact[...], wo_ref[da:, :], preferred_element_type=F32))

    hn = _rms(h1, ng_ref[...]).astype(BF16)
    qx = jnp.dot(hn, wq_ref[...], preferred_element_type=F32).astype(BF16)
    hd = d // N_X_HEADS
    heads = []
    for h in range(N_X_HEADS):
        sl = slice(h * hd, (h + 1) * hd)
        s = lax.dot_general(qx[:, sl], mk_ref[0, :, sl], (((1,), (1,)), ((), ())),
                            preferred_element_type=F32)
        s = s - jnp.max(s, axis=-1, keepdims=True)
        e = jnp.exp2(s)
        p = e / jnp.sum(e, axis=-1, keepdims=True)
        heads.append(jnp.dot(p.astype(BF16), mv_ref[0, :, sl], preferred_element_type=F32))
    o = jnp.concatenate(heads, axis=-1).astype(BF16)
    o_ref[0] = h1 + jnp.dot(o, wco_ref[...], preferred_element_type=F32)


def _mixcross(x, a, c, cv_w, cv_b, ln_g, ln_b, w_out, ng, w_cq, mk, mv, w_co):
    b, s, d = x.shape
    t = min(TOK_BLOCK, s)
    da, cw, m = a.shape[2], c.shape[2], mk.shape[1]
    hpb = t // CV_HALO
    const = lambda shape: pl.BlockSpec(shape, lambda bi, i: (0,) * len(shape))
    tok = lambda w: pl.BlockSpec((1, t, w), lambda bi, i: (bi, i, 0))
    return pl.pallas_call(
        _mixcross_kernel,
        grid=(b, s // t),
        in_specs=[tok(d), tok(da), tok(cw),
                  pl.BlockSpec((1, CV_HALO, cw), lambda bi, i: (bi, jnp.maximum(i * hpb - 1, 0), 0)),
                  const((CV_KERNEL, SUBLANES, cw)), const((1, cw)), const((1, cw)), const((1, cw)),
                  const((da + cw, d)), const((1, d)), const((d, d)),
                  pl.BlockSpec((1, m, d), lambda bi, i: (bi, 0, 0)),
                  pl.BlockSpec((1, m, d), lambda bi, i: (bi, 0, 0)),
                  const((d, d))],
        out_specs=tok(d),
        out_shape=jax.ShapeDtypeStruct((b, s, d), F32),
        scratch_shapes=[pltpu.VMEM((t + CV_HALO, cw), F32),
                        pltpu.VMEM((SUBLANES - 1, t + CV_HALO, cw), F32),
                        pltpu.VMEM((t, cw), BF16)],
        compiler_params=pltpu.CompilerParams(
            dimension_semantics=("parallel", "parallel"), vmem_limit_bytes=VMEM_LIMIT_BYTES),
        name="mixcross",
    )(x, a, c, c, jnp.broadcast_to(cv_w[:, None, :], (CV_KERNEL, SUBLANES, cw)),
      cv_b.reshape(1, cw), ln_g.reshape(1, cw), ln_b.reshape(1, cw),
      w_out.astype(BF16), ng.reshape(1, d), w_cq.astype(BF16), mk, mv, w_co.astype(BF16))


def _convffn_kernel(h_ref, ng_ref, wu_ref, cg_ref, cv_ref, wd_ref, fg_ref, o_ref,
                    hn_scr, acc_scr, ga, va, gb, vb, gcarry, vcarry):
    t = h_ref.shape[1]
    i = pl.program_id(1)
    nchunk = wd_ref.shape[0]
    pad = SUBLANES

    @pl.when(i == 0)
    def _():
        gcarry[...] = jnp.zeros(gcarry.shape, F32)
        vcarry[...] = jnp.zeros(vcarry.shape, F32)

    hn_scr[...] = _rms(h_ref[0], ng_ref[...]).astype(BF16)

    def up(f, gdst, vdst):
        hn = hn_scr[...]
        fc = gdst.shape[1]
        dff = nchunk * fc
        gdst[pad:, :] = jnp.dot(hn, wu_ref[:, f * fc:(f + 1) * fc], preferred_element_type=F32)
        vdst[pad:, :] = jnp.dot(hn, wu_ref[:, dff + f * fc:dff + (f + 1) * fc],
                                preferred_element_type=F32)

    def conv(sbuf, w_ref, f, carry):
        sbuf[0:pad, :] = carry[f]
        carry[f] = sbuf[t:t + pad, :]
        w = w_ref[f]
        out = w[FFN_KERNEL - 1:FFN_KERNEL, :] * sbuf[pad:, :]
        for j in range(FFN_KERNEL - 1):
            off = pad - (FFN_KERNEL - 1) + j
            out = out + w[j:j + 1, :] * sbuf[off:off + t, :]
        return out

    def gate(f, gsrc, vsrc):
        g = conv(gsrc, cg_ref, f, gcarry)
        v = conv(vsrc, cv_ref, f, vcarry)
        return (g * jax.nn.sigmoid(g) * v).astype(BF16)

    up(0, ga, va)
    bufs = ((ga, va), (gb, vb))
    f = 0
    for group in FFN_GROUPS:
        total = None
        for u in range(group):
            if f + 1 < nchunk:
                up(f + 1, *bufs[(f + 1) % 2])
            part = jnp.dot(gate(f, *bufs[f % 2]), wd_ref[f], preferred_element_type=F32)
            total = part if total is None else total + part
            f += 1
        if f == group:
            acc_scr[...] = total
        elif f < nchunk:
            acc_scr[...] += total
    o_ref[0] = _rms(h_ref[0] + acc_scr[...] + total, fg_ref[...])


def _convffn(h, ng, w_up, dw_w, w_down, fg):
    b, s, d = h.shape
    t = min(TOK_BLOCK, s)
    dff = w_down.shape[0]
    fc = FFN_CHUNK
    nchunk = dff // fc
    assert dff % fc == 0 and sum(FFN_GROUPS) == nchunk
    dw = dw_w.reshape(FFN_KERNEL, 2, nchunk, fc).transpose(1, 2, 0, 3)
    wd = w_down.astype(BF16).reshape(nchunk, fc, d)
    const = lambda shape: pl.BlockSpec(shape, lambda bi, i: (0,) * len(shape))
    tok = pl.BlockSpec((1, t, d), lambda bi, i: (bi, i, 0))
    return pl.pallas_call(
        _convffn_kernel,
        grid=(b, s // t),
        in_specs=[tok, const((1, d)), const((d, 2 * dff)),
                  const((nchunk, FFN_KERNEL, fc)), const((nchunk, FFN_KERNEL, fc)),
                  const((nchunk, fc, d)), const((1, d))],
        out_specs=tok,
        out_shape=jax.ShapeDtypeStruct((b, s, d), F32),
        scratch_shapes=[pltpu.VMEM((t, d), BF16), pltpu.VMEM((t, d), F32),
                        pltpu.VMEM((t + SUBLANES, fc), F32), pltpu.VMEM((t + SUBLANES, fc), F32),
                        pltpu.VMEM((t + SUBLANES, fc), F32), pltpu.VMEM((t + SUBLANES, fc), F32),
                        pltpu.VMEM((nchunk, SUBLANES, fc), F32),
                        pltpu.VMEM((nchunk, SUBLANES, fc), F32)],
        compiler_params=pltpu.CompilerParams(
            dimension_semantics=("parallel", "arbitrary"), vmem_limit_bytes=VMEM_LIMIT_BYTES),
        name="convffn",
    )(h, ng.reshape(1, d), w_up.astype(BF16), dw[0], dw[1], wd, fg.reshape(1, d))


def kernel(x, mem, positions, norm_mix_g, w_in, lam_q1, lam_k1, lam_q2, lam_k2, subln_g, cv_dw_w, cv_dw_b, cv_ln_g, cv_ln_b, w_out, norm_cross_g, norm_mem_g, w_cq, w_ckv, w_co, norm_ffn_g, w_up, ffn_dw_w, w_down, norm_final_g):
    assert w_in.shape[0] == 1, "single-layer operation"
    q, k, v, c = _inproj(x, positions, norm_mix_g[0], w_in[0])
    mk, mv = _memkv(mem, norm_mem_g[0], w_ckv[0])
    a = _diff_attention(q, k, v, lam_q1[0], lam_k1[0], lam_q2[0], lam_k2[0], subln_g[0])
    h2 = _mixcross(x, a, c, cv_dw_w[0], cv_dw_b[0], cv_ln_g[0], cv_ln_b[0], w_out[0],
                   norm_cross_g[0], w_cq[0], mk, mv, w_co[0])
    return _convffn(h2, norm_ffn_g[0], w_up[0], ffn_dw_w[0], w_down[0], norm_final_g)
```

```python
import functools
import math

import jax
import jax.numpy as jnp
from jax import lax
from jax.experimental import pallas as pl
from jax.experimental.pallas import tpu as pltpu

F32 = jnp.float32
BF16 = jnp.bfloat16

N_DA_HEADS = 4
DA_VDIM = 128
DA_QKDIM = 64
ROT_DIM = 16
ROPE_THETA = 500000.0
CV_KERNEL = 31
N_X_HEADS = 4
FFN_KERNEL = 3
EPS = 1e-6
LAM_INIT = 0.8 - 0.6 * math.exp(-0.3 * 0)

LANES = 128
SUBLANES = 8
VMEM_LIMIT_BYTES = 56 * 1024 * 1024

NEG_BIG = -1e30

TOK_BLOCK = 512
ATT_UNROLLS = (16, 8, 4, 2)
ATT_BLOCK = 512
CV_HALO = 32
CV_ROW_TILE = 32


def _rms(x, g):
    ms = jnp.mean(x * x, axis=-1, keepdims=True)
    return x * lax.rsqrt(ms + EPS) * g


def _inproj_kernel(x_ref, pos_ref, g_ref, w_ref, freq_ref, sel_ref, one_ref,
                   q_ref, k_ref, v_ref, c_ref):
    da = N_DA_HEADS * DA_VDIM
    xn = _rms(x_ref[0], g_ref[...]).astype(BF16)
    ang = pos_ref[0].astype(F32) * freq_ref[...]

    def split3(v):
        v1 = v.astype(BF16)
        r1 = v - v1.astype(F32)
        v2 = r1.astype(BF16)
        v3 = (r1 - v2.astype(F32)).astype(BF16)
        return [v1, v2, v3]

    trig = jnp.concatenate(split3(jnp.cos(ang)) + split3(jnp.sin(ang)), axis=0)
    tab = lax.dot_general(trig, sel_ref[...], (((0,), (0,)), ((), ())),
                          preferred_element_type=F32)
    cos = tab[:, 0:LANES] + one_ref[...]
    sa = tab[:, LANES:2 * LANES]
    sb = tab[:, 2 * LANES:3 * LANES]

    def rope(t):
        return (t * cos + pltpu.roll(t, ROT_DIM // 2, 1) * sa
                + pltpu.roll(t, LANES - ROT_DIM // 2, 1) * sb)

    cw = (w_ref.shape[1] - 3 * da) // 2
    uv = jnp.dot(xn, w_ref[:, 3 * da:3 * da + cw], preferred_element_type=F32)
    ug = jnp.dot(xn, w_ref[:, 3 * da + cw:], preferred_element_type=F32)
    c_ref[0] = uv * jax.nn.sigmoid(ug)
    qp = jnp.dot(xn, w_ref[:, 0:da], preferred_element_type=F32)
    kp = jnp.dot(xn, w_ref[:, da:2 * da], preferred_element_type=F32)
    scale = math.log2(math.e) / math.sqrt(DA_QKDIM)
    for h in range(N_DA_HEADS):
        sl = slice(h * LANES, (h + 1) * LANES)
        q_ref[0, :, sl] = (rope(qp[:, sl]) * scale).astype(BF16)
        k_ref[0, :, sl] = rope(kp[:, sl]).astype(BF16)
    v_ref[0] = jnp.dot(xn, w_ref[:, 2 * da:3 * da], preferred_element_type=F32).astype(BF16)


def _inproj(x, positions, g, w_in):
    b, s, d = x.shape
    t = min(TOK_BLOCK, s)
    da = N_DA_HEADS * DA_VDIM
    cw = (w_in.shape[1] - 3 * da) // 2
    half = ROT_DIM // 2
    inv_freq = ROPE_THETA ** (-jnp.arange(0, ROT_DIM, 2, dtype=F32) / ROT_DIM)
    freq = jnp.broadcast_to(inv_freq[:, None], (half, t))
    j = jnp.arange(LANES) % DA_QKDIM
    hit = (jnp.arange(half)[:, None] == (j % half)[None, :]) & (j < ROT_DIM)[None, :]
    e_cos = hit.astype(F32)
    e_sa = (hit & (j >= half)[None, :]).astype(F32)
    e_sb = -(hit & (j < half)[None, :]).astype(F32)
    zero = jnp.zeros_like(e_cos)
    sel = jnp.concatenate([jnp.concatenate([e_cos, zero, zero], axis=1)] * 3
                          + [jnp.concatenate([zero, e_sa, e_sb], axis=1)] * 3, axis=0).astype(BF16)
    one = jnp.where(j < ROT_DIM, 0.0, 1.0).astype(F32)[None, :]
    const = lambda shape: pl.BlockSpec(shape, lambda bi, i: (0,) * len(shape))
    tok = lambda w: pl.BlockSpec((1, t, w), lambda bi, i: (bi, i, 0))
    return pl.pallas_call(
        _inproj_kernel,
        grid=(b, s // t),
        in_specs=[tok(d), pl.BlockSpec((1, 1, t), lambda bi, i: (bi * (s // t) + i, 0, 0)),
                  const((1, d)), const(w_in.shape),
                  const((half, t)), const((6 * half, 3 * LANES)), const((1, LANES))],
        out_specs=[tok(da), tok(da), tok(da), tok(cw)],
        out_shape=[jax.ShapeDtypeStruct((b, s, da), BF16)] * 3
                  + [jax.ShapeDtypeStruct((b, s, cw), F32)],
        compiler_params=pltpu.CompilerParams(
            dimension_semantics=("parallel", "parallel"), vmem_limit_bytes=VMEM_LIMIT_BYTES),
        name="inproj",
    )(x, positions.reshape(b * (s // t), 1, t), g.reshape(1, d), w_in.astype(BF16), freq, sel, one)


def _memkv_kernel(m_ref, g_ref, w_ref, k_ref, v_ref):
    d = m_ref.shape[2]
    mn = _rms(m_ref[0], g_ref[...]).astype(BF16)
    kv = jnp.dot(mn, w_ref[...], preferred_element_type=F32)
    scale = math.log2(math.e) / math.sqrt(d // N_X_HEADS)
    k_ref[0] = (kv[:, :d] * scale).astype(BF16)
    v_ref[0] = kv[:, d:].astype(BF16)


def _memkv(mem, g, w_ckv):
    b, m, d = mem.shape
    return pl.pallas_call(
        _memkv_kernel,
        grid=(b,),
        in_specs=[pl.BlockSpec((1, m, d), lambda bi: (bi, 0, 0)),
                  pl.BlockSpec((1, d), lambda bi: (0, 0)),
                  pl.BlockSpec((d, 2 * d), lambda bi: (0, 0))],
        out_specs=[pl.BlockSpec((1, m, d), lambda bi: (bi, 0, 0))] * 2,
        out_shape=[jax.ShapeDtypeStruct((b, m, d), BF16)] * 2,
        compiler_params=pltpu.CompilerParams(
            dimension_semantics=("parallel",), vmem_limit_bytes=VMEM_LIMIT_BYTES),
        name="memkv",
    )(mem, g.reshape(1, d), w_ckv.astype(BF16))


def _attn_kernel(q_ref, qn_ref, k_ref, v_ref, lq1_ref, lk1_ref, lq2_ref, lk2_ref, g_ref, o_ref,
                 qs_scr, sa_scr, sb_scr, m_scr, acc_scr):
    t = ATT_BLOCK
    rows = 2 * t
    qi = pl.program_id(2)
    lane = lax.broadcasted_iota(jnp.int32, (t, LANES), 1)
    ones_col = jnp.where(lane == 0, 1.0, 0.0).astype(BF16)

    def stack_q(q):
        zero = jnp.zeros_like(q)
        qs_scr[0:t, :] = jnp.where(lane < DA_QKDIM, q, zero)
        qs_scr[t:rows, :] = jnp.where(lane >= DA_QKDIM, q, zero)

    def reset_state():
        m_scr[...] = jnp.full(m_scr.shape, NEG_BIG, F32)
        acc_scr[...] = jnp.zeros(acc_scr.shape, F32)

    def scores(j, dst, nsplit=1):
        k0 = pl.multiple_of(j * t, t)
        kc = k_ref[0, pl.ds(k0, t), :]
        rs = rows // nsplit
        for r in range(0, rows, rs):
            dst[r:r + rs, :] = lax.dot_general(qs_scr[r:r + rs, :], kc,
                                               (((1,), (1,)), ((), ())), preferred_element_type=F32)

    def consume(j, src, masked, nsplit=1):
        k0 = pl.multiple_of(j * t, t)
        vc = jnp.concatenate([v_ref[0, pl.ds(k0, t), :], ones_col], axis=1)
        s = src[...]
        if masked:
            row = lax.broadcasted_iota(jnp.int32, (rows, t), 0) % t
            col = lax.broadcasted_iota(jnp.int32, (rows, t), 1)
            s = jnp.where(col <= row, s, NEG_BIG)
        m_old = m_scr[...]
        m_new = jnp.maximum(m_old, jnp.max(s, axis=1, keepdims=True))
        alpha = jnp.exp2(m_old - m_new)
        p = jnp.exp2(s - jnp.tile(m_new, (1, t // LANES))).astype(BF16)
        rs = rows // nsplit
        for r in range(0, rows, rs):
            acc_scr[r:r + rs, :] = (jnp.tile(alpha[r:r + rs], (1, 2)) * acc_scr[r:r + rs, :]
                                    + jnp.dot(p[r:r + rs], vc, preferred_element_type=F32))
        m_scr[...] = m_new

    @pl.when(qi == 0)
    def _():
        reset_state()
        stack_q(q_ref[0])
        scores(0, sa_scr, nsplit=2)

    def run(j, nblk):
        for u in range(nblk):
            src, dst = (sa_scr, sb_scr) if u % 2 == 0 else (sb_scr, sa_scr)
            scores(j + u + 1, dst)
            consume(j + u, src, masked=False)

    done = 0
    for unroll in ATT_UNROLLS:
        def body(i, carry, unroll=unroll, done=done):
            run(done + unroll * i, unroll)
            return carry

        trips = (qi - done) // unroll
        lax.fori_loop(0, trips, body, 0)
        done = done + trips * unroll

    @pl.when(qi % 2 == 1)
    def _():
        scores(qi, sb_scr)
        consume(qi - 1, sa_scr, masked=False)
        consume(qi, sb_scr, masked=True, nsplit=2)

    @pl.when(qi % 2 == 0)
    def _():
        consume(qi, sa_scr, masked=True, nsplit=2)

    lam = (jnp.exp(jnp.sum(lq1_ref[...] * lk1_ref[...], keepdims=True))
           - jnp.exp(jnp.sum(lq2_ref[...] * lk2_ref[...], keepdims=True)) + LAM_INIT)
    acc = acc_scr[...]
    o = acc[:, :DA_VDIM] / acc[:, DA_VDIM:DA_VDIM + 1]
    a = o[:t] - lam * o[t:]
    o_ref[0] = (_rms(a, g_ref[...]) * (1.0 - LAM_INIT)).astype(o_ref.dtype)

    reset_state()
    stack_q(qn_ref[0])
    scores(0, sa_scr, nsplit=2)


def _diff_attention(q, k, v, lq1, lk1, lq2, lk2, subln_g):
    b, s, _ = q.shape
    tq = ATT_BLOCK
    assert s % tq == 0
    nq = s // tq
    vec = lambda n: pl.BlockSpec((1, n), lambda bi, h, i: (0, 0))
    return pl.pallas_call(
        _attn_kernel,
        grid=(b, N_DA_HEADS, s // tq),
        in_specs=[pl.BlockSpec((1, tq, LANES), lambda bi, h, i: (bi, i, h)),
                  pl.BlockSpec((1, tq, LANES), lambda bi, h, i: (bi, jnp.minimum(i + 1, nq - 1), h)),
                  pl.BlockSpec((1, s, LANES), lambda bi, h, i: (bi, 0, h)),
                  pl.BlockSpec((1, s, DA_VDIM), lambda bi, h, i: (bi, 0, h)),
                  vec(DA_QKDIM), vec(DA_QKDIM), vec(DA_QKDIM), vec(DA_QKDIM), vec(DA_VDIM)],
        out_specs=pl.BlockSpec((1, tq, DA_VDIM), lambda bi, h, i: (bi, i, h)),
        out_shape=jax.ShapeDtypeStruct((b, s, N_DA_HEADS * DA_VDIM), BF16),
        scratch_shapes=[pltpu.VMEM((2 * tq, LANES), BF16),
                        pltpu.VMEM((2 * tq, tq), F32), pltpu.VMEM((2 * tq, tq), F32),
                        pltpu.VMEM((2 * tq, LANES), F32),
                        pltpu.VMEM((2 * tq, 2 * DA_VDIM), F32)],
        compiler_params=pltpu.CompilerParams(
            dimension_semantics=("parallel", "parallel", "arbitrary"),
            vmem_limit_bytes=VMEM_LIMIT_BYTES),
        name="diffattn",
    )(q, q, k, v, lq1.reshape(1, -1), lk1.reshape(1, -1), lq2.reshape(1, -1), lk2.reshape(1, -1),
      subln_g.reshape(1, -1))


def _mixcross_kernel(x_ref, a_ref, c_ref, halo_ref, cw_ref, cb_ref, lg_ref, lb_ref, wo_ref,
                     ng_ref, wq_ref, mk_ref, mv_ref, wco_ref, o_ref, buf, shf, cact):
    t = x_ref.shape[1]
    d = x_ref.shape[2]
    i = pl.program_id(1)
    n = t + CV_HALO
    halo = halo_ref[0]
    buf[0:CV_HALO, :] = jnp.where(i > 0, halo, jnp.zeros_like(halo))
    buf[CV_HALO:, :] = c_ref[0]
    for p in range(1, SUBLANES):
        shf[p - 1, SUBLANES:n, :] = buf[SUBLANES - p:n - p, :]

    for r0 in range(0, t, CV_ROW_TILE):
        acc = jnp.zeros((CV_ROW_TILE, buf.shape[1]), F32) + cb_ref[...]
        for dl in range(CV_KERNEL):
            al, p = divmod(dl, SUBLANES)
            row = r0 + CV_HALO - al * SUBLANES
            src = buf if p == 0 else shf.at[p - 1]
            wj = CV_KERNEL - 1 - dl
            wt = jnp.tile(cw_ref[wj], (CV_ROW_TILE // SUBLANES, 1))
            acc = acc + wt * src[row:row + CV_ROW_TILE, :]
        mu = jnp.mean(acc, axis=-1, keepdims=True)
        xc = acc - mu
        var = jnp.mean(xc * xc, axis=-1, keepdims=True)
        y = xc * lax.rsqrt(var + EPS) * lg_ref[...] + lb_ref[...]
        cact[r0:r0 + CV_ROW_TILE, :] = (y * jax.nn.sigmoid(y)).astype(BF16)

    da = a_ref.shape[2]
    h1 = (x_ref[0]
          + jnp.dot(a_ref[0], wo_ref[0:da, :], preferred_element_type=F32)
          + jnp.dot(cact[...], wo_ref[da:, :], preferred_element_type=F32))

    hn = _rms(h1, ng_ref[...]).astype(BF16)
    qx = jnp.dot(hn, wq_ref[...], preferred_element_type=F32).astype(BF16)
    hd = d // N_X_HEADS
    heads = []
    for h in range(N_X_HEADS):
        sl = slice(h * hd, (h + 1) * hd)
        s = lax.dot_general(qx[:, sl], mk_ref[0, :, sl], (((1,), (1,)), ((), ())),
                            preferred_element_type=F32)
        s = s - jnp.max(s, axis=-1, keepdims=True)
        e = jnp.exp2(s)
        p = e / jnp.sum(e, axis=-1, keepdims=True)
        heads.append(jnp.dot(p.astype(BF16), mv_ref[0, :, sl], preferred_element_type=F32))
    o = jnp.concatenate(heads, axis=-1).astype(BF16)
    o_ref[0] = h1 + jnp.dot(o, wco_ref[...], preferred_element_type=F32)


def _mixcross(x, a, c, cv_w, cv_b, ln_g, ln_b, w_out, ng, w_cq, mk, mv, w_co):
    b, s, d = x.shape
    t = min(TOK_BLOCK, s)
    da, cw, m = a.shape[2], c.shape[2], mk.shape[1]
    hpb = t // CV_HALO
    const = lambda shape: pl.BlockSpec(shape, lambda bi, i: (0,) * len(shape))
    tok = lambda w: pl.BlockSpec((1, t, w), lambda bi, i: (bi, i, 0))
    return pl.pallas_call(
        _mixcross_kernel,
        grid=(b, s // t),
        in_specs=[tok(d), tok(da), tok(cw),
                  pl.BlockSpec((1, CV_HALO, cw), lambda bi, i: (bi, jnp.maximum(i * hpb - 1, 0), 0)),
                  const((CV_KERNEL, SUBLANES, cw)), const((1, cw)), const((1, cw)), const((1, cw)),
                  const((da + cw, d)), const((1, d)), const((d, d)),
                  pl.BlockSpec((1, m, d), lambda bi, i: (bi, 0, 0)),
                  pl.BlockSpec((1, m, d), lambda bi, i: (bi, 0, 0)),
                  const((d, d))],
        out_specs=tok(d),
        out_shape=jax.ShapeDtypeStruct((b, s, d), F32),
        scratch_shapes=[pltpu.VMEM((t + CV_HALO, cw), F32),
                        pltpu.VMEM((SUBLANES - 1, t + CV_HALO, cw), F32),
                        pltpu.VMEM((t, cw), BF16)],
        compiler_params=pltpu.CompilerParams(
            dimension_semantics=("parallel", "parallel"), vmem_limit_bytes=VMEM_LIMIT_BYTES),
        name="mixcross",
    )(x, a, c, c, jnp.broadcast_to(cv_w[:, None, :], (CV_KERNEL, SUBLANES, cw)),
      cv_b.reshape(1, cw), ln_g.reshape(1, cw), ln_b.reshape(1, cw),
      w_out.astype(BF16), ng.reshape(1, d), w_cq.astype(BF16), mk, mv, w_co.astype(BF16))


def _convffn_kernel(h_ref, ng_ref, wu_ref, cg_ref, cv_ref, wd_ref, fg_ref, o_ref,
                    gbuf, vbuf, gcarry, vcarry):
    t = h_ref.shape[1]
    dff = wd_ref.shape[0]
    pad = SUBLANES

    @pl.when(pl.program_id(1) == 0)
    def _():
        gcarry[...] = jnp.zeros(gcarry.shape, F32)
        vcarry[...] = jnp.zeros(vcarry.shape, F32)

    h = h_ref[0]
    hn = _rms(h, ng_ref[...]).astype(BF16)
    gbuf[pad:, :] = jnp.dot(hn, wu_ref[:, 0:dff], preferred_element_type=F32)
    vbuf[pad:, :] = jnp.dot(hn, wu_ref[:, dff:2 * dff], preferred_element_type=F32)

    def conv(sbuf, w_ref, carry):
        sbuf[0:pad, :] = carry[...]
        carry[...] = sbuf[t:t + pad, :]
        out = w_ref[FFN_KERNEL - 1:FFN_KERNEL, :] * sbuf[pad:, :]
        for j in range(FFN_KERNEL - 1):
            off = pad - (FFN_KERNEL - 1) + j
            out = out + w_ref[j:j + 1, :] * sbuf[off:off + t, :]
        return out

    g = conv(gbuf, cg_ref, gcarry)
    v = conv(vbuf, cv_ref, vcarry)
    z = (g * jax.nn.sigmoid(g) * v).astype(BF16)
    o_ref[0] = _rms(h + jnp.dot(z, wd_ref[...], preferred_element_type=F32), fg_ref[...])


def _convffn(h, ng, w_up, dw_w, w_down, fg):
    b, s, d = h.shape
    t = min(TOK_BLOCK, s)
    dff = w_down.shape[0]
    const = lambda shape: pl.BlockSpec(shape, lambda bi, i: (0,) * len(shape))
    tok = pl.BlockSpec((1, t, d), lambda bi, i: (bi, i, 0))
    return pl.pallas_call(
        _convffn_kernel,
        grid=(b, s // t),
        in_specs=[tok, const((1, d)), const((d, 2 * dff)),
                  const((FFN_KERNEL, dff)), const((FFN_KERNEL, dff)),
                  const((dff, d)), const((1, d))],
        out_specs=tok,
        out_shape=jax.ShapeDtypeStruct((b, s, d), F32),
        scratch_shapes=[pltpu.VMEM((t + SUBLANES, dff), F32), pltpu.VMEM((t + SUBLANES, dff), F32),
                        pltpu.VMEM((SUBLANES, dff), F32), pltpu.VMEM((SUBLANES, dff), F32)],
        compiler_params=pltpu.CompilerParams(
            dimension_semantics=("parallel", "arbitrary"), vmem_limit_bytes=VMEM_LIMIT_BYTES),
        name="convffn",
    )(h, ng.reshape(1, d), w_up.astype(BF16), dw_w[:, :dff], dw_w[:, dff:], w_down.astype(BF16),
      fg.reshape(1, d))


def kernel(x, mem, positions, norm_mix_g, w_in, lam_q1, lam_k1, lam_q2, lam_k2, subln_g, cv_dw_w, cv_dw_b, cv_ln_g, cv_ln_b, w_out, norm_cross_g, norm_mem_g, w_cq, w_ckv, w_co, norm_ffn_g, w_up, ffn_dw_w, w_down, norm_final_g):
    assert w_in.shape[0] == 1, "single-layer operation"
    q, k, v, c = _inproj(x, positions, norm_mix_g[0], w_in[0])
    mk, mv = _memkv(mem, norm_mem_g[0], w_ckv[0])
    a = _diff_attention(q, k, v, lam_q1[0], lam_k1[0], lam_q2[0], lam_k2[0], subln_g[0])
    h2 = _mixcross(x, a, c, cv_dw_w[0], cv_dw_b[0], cv_ln_g[0], cv_ln_b[0], w_out[0],
                   norm_cross_g[0], w_cq[0], mk, mv, w_co[0])
    return _convffn(h2, norm_ffn_g[0], w_up[0], ffn_dw_w[0], w_down[0], norm_final_g)
```

```python
import functools
import math

import jax
import jax.numpy as jnp
from jax import lax
from jax.experimental import pallas as pl
from jax.experimental.pallas import tpu as pltpu

F32 = jnp.float32
BF16 = jnp.bfloat16

N_DA_HEADS = 4
DA_VDIM = 128
DA_QKDIM = 64
ROT_DIM = 16
ROPE_THETA = 500000.0
CV_KERNEL = 31
N_X_HEADS = 4
FFN_KERNEL = 3
EPS = 1e-6
LAM_INIT = 0.8 - 0.6 * math.exp(-0.3 * 0)

LANES = 128
SUBLANES = 8
VMEM_LIMIT_BYTES = 56 * 1024 * 1024

NEG_BIG = -1e30

TOK_BLOCK = 512
INPROJ_TOK_BLOCK = 1024
ATT_UNROLLS = (16, 8, 4, 2)
ATT_BLOCK = 512
CV_HALO = 32
CV_ROW_TILE = 32


def _rms(x, g):
    ms = jnp.mean(x * x, axis=-1, keepdims=True)
    return x * lax.rsqrt(ms + EPS) * g


def _inproj_kernel(x_ref, pos_ref, g_ref, w_ref, freq_ref, sel_ref, one_ref,
                   q_ref, k_ref, v_ref, c_ref):
    da = N_DA_HEADS * DA_VDIM
    xn = _rms(x_ref[0], g_ref[...]).astype(BF16)
    ang = pos_ref[0].astype(F32) * freq_ref[...]

    def split3(v):
        v1 = v.astype(BF16)
        r1 = v - v1.astype(F32)
        v2 = r1.astype(BF16)
        v3 = (r1 - v2.astype(F32)).astype(BF16)
        return [v1, v2, v3]

    trig = jnp.concatenate(split3(jnp.cos(ang)) + split3(jnp.sin(ang)), axis=0)
    tab = lax.dot_general(trig, sel_ref[...], (((0,), (0,)), ((), ())),
                          preferred_element_type=F32)
    cos = tab[:, 0:LANES] + one_ref[...]
    sa = tab[:, LANES:2 * LANES]
    sb = tab[:, 2 * LANES:3 * LANES]

    def rope(t):
        return (t * cos + pltpu.roll(t, ROT_DIM // 2, 1) * sa
                + pltpu.roll(t, LANES - ROT_DIM // 2, 1) * sb)

    cw = (w_ref.shape[1] - 3 * da) // 2
    uv = jnp.dot(xn, w_ref[:, 3 * da:3 * da + cw], preferred_element_type=F32)
    ug = jnp.dot(xn, w_ref[:, 3 * da + cw:], preferred_element_type=F32)
    c_ref[0] = uv * jax.nn.sigmoid(ug)
    qp = jnp.dot(xn, w_ref[:, 0:da], preferred_element_type=F32)
    kp = jnp.dot(xn, w_ref[:, da:2 * da], preferred_element_type=F32)
    scale = math.log2(math.e) / math.sqrt(DA_QKDIM)
    for h in range(N_DA_HEADS):
        sl = slice(h * LANES, (h + 1) * LANES)
        q_ref[0, :, sl] = (rope(qp[:, sl]) * scale).astype(BF16)
        k_ref[0, :, sl] = rope(kp[:, sl]).astype(BF16)
    v_ref[0] = jnp.dot(xn, w_ref[:, 2 * da:3 * da], preferred_element_type=F32).astype(BF16)


def _inproj(x, positions, g, w_in):
    b, s, d = x.shape
    t = min(INPROJ_TOK_BLOCK, s)
    da = N_DA_HEADS * DA_VDIM
    cw = (w_in.shape[1] - 3 * da) // 2
    half = ROT_DIM // 2
    inv_freq = ROPE_THETA ** (-jnp.arange(0, ROT_DIM, 2, dtype=F32) / ROT_DIM)
    freq = jnp.broadcast_to(inv_freq[:, None], (half, t))
    j = jnp.arange(LANES) % DA_QKDIM
    hit = (jnp.arange(half)[:, None] == (j % half)[None, :]) & (j < ROT_DIM)[None, :]
    e_cos = hit.astype(F32)
    e_sa = (hit & (j >= half)[None, :]).astype(F32)
    e_sb = -(hit & (j < half)[None, :]).astype(F32)
    zero = jnp.zeros_like(e_cos)
    sel = jnp.concatenate([jnp.concatenate([e_cos, zero, zero], axis=1)] * 3
                          + [jnp.concatenate([zero, e_sa, e_sb], axis=1)] * 3, axis=0).astype(BF16)
    one = jnp.where(j < ROT_DIM, 0.0, 1.0).astype(F32)[None, :]
    const = lambda shape: pl.BlockSpec(shape, lambda bi, i: (0,) * len(shape))
    tok = lambda w: pl.BlockSpec((1, t, w), lambda bi, i: (bi, i, 0))
    return pl.pallas_call(
        _inproj_kernel,
        grid=(b, s // t),
        in_specs=[tok(d), pl.BlockSpec((1, 1, t), lambda bi, i: (bi * (s // t) + i, 0, 0)),
                  const((1, d)), const(w_in.shape),
                  const((half, t)), const((6 * half, 3 * LANES)), const((1, LANES))],
        out_specs=[tok(da), tok(da), tok(da), tok(cw)],
        out_shape=[jax.ShapeDtypeStruct((b, s, da), BF16)] * 3
                  + [jax.ShapeDtypeStruct((b, s, cw), F32)],
        compiler_params=pltpu.CompilerParams(
            dimension_semantics=("parallel", "parallel"), vmem_limit_bytes=VMEM_LIMIT_BYTES),
        name="inproj",
    )(x, positions.reshape(b * (s // t), 1, t), g.reshape(1, d), w_in.astype(BF16), freq, sel, one)


def _memkv_kernel(m_ref, g_ref, w_ref, k_ref, v_ref):
    d = m_ref.shape[2]
    mn = _rms(m_ref[0], g_ref[...]).astype(BF16)
    kv = jnp.dot(mn, w_ref[...], preferred_element_type=F32)
    scale = math.log2(math.e) / math.sqrt(d // N_X_HEADS)
    k_ref[0] = (kv[:, :d] * scale).astype(BF16)
    v_ref[0] = kv[:, d:].astype(BF16)


def _memkv(mem, g, w_ckv):
    b, m, d = mem.shape
    return pl.pallas_call(
        _memkv_kernel,
        grid=(b,),
        in_specs=[pl.BlockSpec((1, m, d), lambda bi: (bi, 0, 0)),
                  pl.BlockSpec((1, d), lambda bi: (0, 0)),
                  pl.BlockSpec((d, 2 * d), lambda bi: (0, 0))],
        out_specs=[pl.BlockSpec((1, m, d), lambda bi: (bi, 0, 0))] * 2,
        out_shape=[jax.ShapeDtypeStruct((b, m, d), BF16)] * 2,
        compiler_params=pltpu.CompilerParams(
            dimension_semantics=("parallel",), vmem_limit_bytes=VMEM_LIMIT_BYTES),
        name="memkv",
    )(mem, g.reshape(1, d), w_ckv.astype(BF16))


def _attn_kernel(q_ref, qn_ref, k_ref, v_ref, lq1_ref, lk1_ref, lq2_ref, lk2_ref, g_ref, o_ref,
                 qs_scr, sa_scr, sb_scr, m_scr, acc_scr):
    t = ATT_BLOCK
    rows = 2 * t
    qi = pl.program_id(2)
    lane = lax.broadcasted_iota(jnp.int32, (t, LANES), 1)
    ones_col = jnp.where(lane == 0, 1.0, 0.0).astype(BF16)

    def stack_q(q):
        zero = jnp.zeros_like(q)
        qs_scr[0:t, :] = jnp.where(lane < DA_QKDIM, q, zero)
        qs_scr[t:rows, :] = jnp.where(lane >= DA_QKDIM, q, zero)

    def reset_state():
        m_scr[...] = jnp.full(m_scr.shape, NEG_BIG, F32)
        acc_scr[...] = jnp.zeros(acc_scr.shape, F32)

    def scores(j, dst, nsplit=1):
        k0 = pl.multiple_of(j * t, t)
        kc = k_ref[0, pl.ds(k0, t), :]
        rs = rows // nsplit
        for r in range(0, rows, rs):
            dst[r:r + rs, :] = lax.dot_general(qs_scr[r:r + rs, :], kc,
                                               (((1,), (1,)), ((), ())), preferred_element_type=F32)

    def consume(j, src, masked, nsplit=1):
        k0 = pl.multiple_of(j * t, t)
        vc = jnp.concatenate([v_ref[0, pl.ds(k0, t), :], ones_col], axis=1)
        s = src[...]
        if masked:
            row = lax.broadcasted_iota(jnp.int32, (rows, t), 0) % t
            col = lax.broadcasted_iota(jnp.int32, (rows, t), 1)
            s = jnp.where(col <= row, s, NEG_BIG)
        m_old = m_scr[...]
        m_new = jnp.maximum(m_old, jnp.max(s, axis=1, keepdims=True))
        alpha = jnp.exp2(m_old - m_new)
        p = jnp.exp2(s - jnp.tile(m_new, (1, t // LANES))).astype(BF16)
        rs = rows // nsplit
        for r in range(0, rows, rs):
            acc_scr[r:r + rs, :] = (jnp.tile(alpha[r:r + rs], (1, 2)) * acc_scr[r:r + rs, :]
                                    + jnp.dot(p[r:r + rs], vc, preferred_element_type=F32))
        m_scr[...] = m_new

    @pl.when(qi == 0)
    def _():
        reset_state()
        stack_q(q_ref[0])
        scores(0, sa_scr, nsplit=2)

    def run(j, nblk):
        for u in range(nblk):
            src, dst = (sa_scr, sb_scr) if u % 2 == 0 else (sb_scr, sa_scr)
            scores(j + u + 1, dst)
            consume(j + u, src, masked=False)

    done = 0
    for unroll in ATT_UNROLLS:
        def body(i, carry, unroll=unroll, done=done):
            run(done + unroll * i, unroll)
            return carry

        trips = (qi - done) // unroll
        lax.fori_loop(0, trips, body, 0)
        done = done + trips * unroll

    @pl.when(qi % 2 == 1)
    def _():
        scores(qi, sb_scr)
        consume(qi - 1, sa_scr, masked=False)
        consume(qi, sb_scr, masked=True, nsplit=2)

    @pl.when(qi % 2 == 0)
    def _():
        consume(qi, sa_scr, masked=True, nsplit=2)

    lam = (jnp.exp(jnp.sum(lq1_ref[...] * lk1_ref[...], keepdims=True))
           - jnp.exp(jnp.sum(lq2_ref[...] * lk2_ref[...], keepdims=True)) + LAM_INIT)
    acc = acc_scr[...]
    o = acc[:, :DA_VDIM] / acc[:, DA_VDIM:DA_VDIM + 1]
    a = o[:t] - lam * o[t:]
    o_ref[0] = (_rms(a, g_ref[...]) * (1.0 - LAM_INIT)).astype(o_ref.dtype)

    reset_state()
    stack_q(qn_ref[0])
    scores(0, sa_scr, nsplit=2)


def _diff_attention(q, k, v, lq1, lk1, lq2, lk2, subln_g):
    b, s, _ = q.shape
    tq = ATT_BLOCK
    assert s % tq == 0
    nq = s // tq
    vec = lambda n: pl.BlockSpec((1, n), lambda bi, h, i: (0, 0))
    return pl.pallas_call(
        _attn_kernel,
        grid=(b, N_DA_HEADS, s // tq),
        in_specs=[pl.BlockSpec((1, tq, LANES), lambda bi, h, i: (bi, i, h)),
                  pl.BlockSpec((1, tq, LANES), lambda bi, h, i: (bi, jnp.minimum(i + 1, nq - 1), h)),
                  pl.BlockSpec((1, s, LANES), lambda bi, h, i: (bi, 0, h)),
                  pl.BlockSpec((1, s, DA_VDIM), lambda bi, h, i: (bi, 0, h)),
                  vec(DA_QKDIM), vec(DA_QKDIM), vec(DA_QKDIM), vec(DA_QKDIM), vec(DA_VDIM)],
        out_specs=pl.BlockSpec((1, tq, DA_VDIM), lambda bi, h, i: (bi, i, h)),
        out_shape=jax.ShapeDtypeStruct((b, s, N_DA_HEADS * DA_VDIM), BF16),
        scratch_shapes=[pltpu.VMEM((2 * tq, LANES), BF16),
                        pltpu.VMEM((2 * tq, tq), F32), pltpu.VMEM((2 * tq, tq), F32),
                        pltpu.VMEM((2 * tq, LANES), F32),
                        pltpu.VMEM((2 * tq, 2 * DA_VDIM), F32)],
        compiler_params=pltpu.CompilerParams(
            dimension_semantics=("parallel", "parallel", "arbitrary"),
            vmem_limit_bytes=VMEM_LIMIT_BYTES),
        name="diffattn",
    )(q, q, k, v, lq1.reshape(1, -1), lk1.reshape(1, -1), lq2.reshape(1, -1), lk2.reshape(1, -1),
      subln_g.reshape(1, -1))


def _mixcross_kernel(x_ref, a_ref, c_ref, halo_ref, cw_ref, cb_ref, lg_ref, lb_ref, wo_ref,
                     ng_ref, wq_ref, mk_ref, mv_ref, wco_ref, o_ref, buf, shf, cact):
    t = x_ref.shape[1]
    d = x_ref.shape[2]
    i = pl.program_id(1)
    n = t + CV_HALO
    halo = halo_ref[0]
    buf[0:CV_HALO, :] = jnp.where(i > 0, halo, jnp.zeros_like(halo))
    buf[CV_HALO:, :] = c_ref[0]
    for p in range(1, SUBLANES):
        shf[p - 1, SUBLANES:n, :] = buf[SUBLANES - p:n - p, :]

    for r0 in range(0, t, CV_ROW_TILE):
        acc = jnp.zeros((CV_ROW_TILE, buf.shape[1]), F32) + cb_ref[...]
        for dl in range(CV_KERNEL):
            al, p = divmod(dl, SUBLANES)
            row = r0 + CV_HALO - al * SUBLANES
            src = buf if p == 0 else shf.at[p - 1]
            wj = CV_KERNEL - 1 - dl
            wt = jnp.tile(cw_ref[wj], (CV_ROW_TILE // SUBLANES, 1))
            acc = acc + wt * src[row:row + CV_ROW_TILE, :]
        mu = jnp.mean(acc, axis=-1, keepdims=True)
        xc = acc - mu
        var = jnp.mean(xc * xc, axis=-1, keepdims=True)
        y = xc * lax.rsqrt(var + EPS) * lg_ref[...] + lb_ref[...]
        cact[r0:r0 + CV_ROW_TILE, :] = (y * jax.nn.sigmoid(y)).astype(BF16)

    da = a_ref.shape[2]
    h1 = (x_ref[0]
          + jnp.dot(a_ref[0], wo_ref[0:da, :], preferred_element_type=F32)
          + jnp.dot(cact[...], wo_ref[da:, :], preferred_element_type=F32))

    hn = _rms(h1, ng_ref[...]).astype(BF16)
    qx = jnp.dot(hn, wq_ref[...], preferred_element_type=F32).astype(BF16)
    hd = d // N_X_HEADS
    heads = []
    for h in range(N_X_HEADS):
        sl = slice(h * hd, (h + 1) * hd)
        s = lax.dot_general(qx[:, sl], mk_ref[0, :, sl], (((1,), (1,)), ((), ())),
                            preferred_element_type=F32)
        s = s - jnp.max(s, axis=-1, keepdims=True)
        e = jnp.exp2(s)
        p = e / jnp.sum(e, axis=-1, keepdims=True)
        heads.append(jnp.dot(p.astype(BF16), mv_ref[0, :, sl], preferred_element_type=F32))
    o = jnp.concatenate(heads, axis=-1).astype(BF16)
    o_ref[0] = h1 + jnp.dot(o, wco_ref[...], preferred_element_type=F32)


def _mixcross(x, a, c, cv_w, cv_b, ln_g, ln_b, w_out, ng, w_cq, mk, mv, w_co):
    b, s, d = x.shape
    t = min(TOK_BLOCK, s)
    da, cw, m = a.shape[2], c.shape[2], mk.shape[1]
    hpb = t // CV_HALO
    const = lambda shape: pl.BlockSpec(shape, lambda bi, i: (0,) * len(shape))
    tok = lambda w: pl.BlockSpec((1, t, w), lambda bi, i: (bi, i, 0))
    return pl.pallas_call(
        _mixcross_kernel,
        grid=(b, s // t),
        in_specs=[tok(d), tok(da), tok(cw),
                  pl.BlockSpec((1, CV_HALO, cw), lambda bi, i: (bi, jnp.maximum(i * hpb - 1, 0), 0)),
                  const((CV_KERNEL, SUBLANES, cw)), const((1, cw)), const((1, cw)), const((1, cw)),
                  const((da + cw, d)), const((1, d)), const((d, d)),
                  pl.BlockSpec((1, m, d), lambda bi, i: (bi, 0, 0)),
                  pl.BlockSpec((1, m, d), lambda bi, i: (bi, 0, 0)),
                  const((d, d))],
        out_specs=tok(d),
        out_shape=jax.ShapeDtypeStruct((b, s, d), F32),
        scratch_shapes=[pltpu.VMEM((t + CV_HALO, cw), F32),
                        pltpu.VMEM((SUBLANES - 1, t + CV_HALO, cw), F32),
                        pltpu.VMEM((t, cw), BF16)],
        compiler_params=pltpu.CompilerParams(
            dimension_semantics=("parallel", "parallel"), vmem_limit_bytes=VMEM_LIMIT_BYTES),
        name="mixcross",
    )(x, a, c, c, jnp.broadcast_to(cv_w[:, None, :], (CV_KERNEL, SUBLANES, cw)),
      cv_b.reshape(1, cw), ln_g.reshape(1, cw), ln_b.reshape(1, cw),
      w_out.astype(BF16), ng.reshape(1, d), w_cq.astype(BF16), mk, mv, w_co.astype(BF16))


def _convffn_kernel(h_ref, ng_ref, wu_ref, cg_ref, cv_ref, wd_ref, fg_ref, o_ref,
                    gbuf, vbuf, gcarry, vcarry):
    t = h_ref.shape[1]
    dff = wd_ref.shape[0]
    pad = SUBLANES

    @pl.when(pl.program_id(1) == 0)
    def _():
        gcarry[...] = jnp.zeros(gcarry.shape, F32)
        vcarry[...] = jnp.zeros(vcarry.shape, F32)

    h = h_ref[0]
    hn = _rms(h, ng_ref[...]).astype(BF16)
    gbuf[pad:, :] = jnp.dot(hn, wu_ref[:, 0:dff], preferred_element_type=F32)
    vbuf[pad:, :] = jnp.dot(hn, wu_ref[:, dff:2 * dff], preferred_element_type=F32)

    def conv(sbuf, w_ref, carry):
        sbuf[0:pad, :] = carry[...]
        carry[...] = sbuf[t:t + pad, :]
        out = w_ref[FFN_KERNEL - 1:FFN_KERNEL, :] * sbuf[pad:, :]
        for j in range(FFN_KERNEL - 1):
            off = pad - (FFN_KERNEL - 1) + j
            out = out + w_ref[j:j + 1, :] * sbuf[off:off + t, :]
        return out

    g = conv(gbuf, cg_ref, gcarry)
    v = conv(vbuf, cv_ref, vcarry)
    z = (g * jax.nn.sigmoid(g) * v).astype(BF16)
    o_ref[0] = _rms(h + jnp.dot(z, wd_ref[...], preferred_element_type=F32), fg_ref[...])


def _convffn(h, ng, w_up, dw_w, w_down, fg):
    b, s, d = h.shape
    t = min(TOK_BLOCK, s)
    dff = w_down.shape[0]
    const = lambda shape: pl.BlockSpec(shape, lambda bi, i: (0,) * len(shape))
    tok = pl.BlockSpec((1, t, d), lambda bi, i: (bi, i, 0))
    return pl.pallas_call(
        _convffn_kernel,
        grid=(b, s // t),
        in_specs=[tok, const((1, d)), const((d, 2 * dff)),
                  const((FFN_KERNEL, dff)), const((FFN_KERNEL, dff)),
                  const((dff, d)), const((1, d))],
        out_specs=tok,
        out_shape=jax.ShapeDtypeStruct((b, s, d), F32),
        scratch_shapes=[pltpu.VMEM((t + SUBLANES, dff), F32), pltpu.VMEM((t + SUBLANES, dff), F32),
                        pltpu.VMEM((SUBLANES, dff), F32), pltpu.VMEM((SUBLANES, dff), F32)],
        compiler_params=pltpu.CompilerParams(
            dimension_semantics=("parallel", "arbitrary"), vmem_limit_bytes=VMEM_LIMIT_BYTES),
        name="convffn",
    )(h, ng.reshape(1, d), w_up.astype(BF16), dw_w[:, :dff], dw_w[:, dff:], w_down.astype(BF16),
      fg.reshape(1, d))


def kernel(x, mem, positions, norm_mix_g, w_in, lam_q1, lam_k1, lam_q2, lam_k2, subln_g, cv_dw_w, cv_dw_b, cv_ln_g, cv_ln_b, w_out, norm_cross_g, norm_mem_g, w_cq, w_ckv, w_co, norm_ffn_g, w_up, ffn_dw_w, w_down, norm_final_g):
    assert w_in.shape[0] == 1, "single-layer operation"
    q, k, v, c = _inproj(x, positions, norm_mix_g[0], w_in[0])
    mk, mv = _memkv(mem, norm_mem_g[0], w_ckv[0])
    a = _diff_attention(q, k, v, lam_q1[0], lam_k1[0], lam_q2[0], lam_k2[0], subln_g[0])
    h2 = _mixcross(x, a, c, cv_dw_w[0], cv_dw_b[0], cv_ln_g[0], cv_ln_b[0], w_out[0],
                   norm_cross_g[0], w_cq[0], mk, mv, w_co[0])
    return _convffn(h2, norm_ffn_g[0], w_up[0], ffn_dw_w[0], w_down[0], norm_final_g)
```
